```python
import jax, jax.numpy as jnp
from jax import lax
import numpy as np

D_MODEL = 1024
BATCH = 4
SEQ = 4096
DEPTH = 4

CHUNK = 64
D_MIX = D_MODEL
HEAD_DIM = 64
RWKV_DIM = D_MIX // 4
RWKV_HEADS = RWKV_DIM // HEAD_DIM
RWKV_W_LORA = 64
RWKV_A_LORA = 64
RWKV_G_LORA = 128
RWKV_GN_EPS = 64e-5
POOL_DIM = D_MIX // 4
POOL_WINDOWS = (2, 4, 8, 16)
POOL_GROUP = POOL_DIM // len(POOL_WINDOWS)
FOX_DIM = D_MIX - RWKV_DIM - POOL_DIM
FOX_HEADS = FOX_DIM // HEAD_DIM
QUERY_BLOCK = 128
N_EXPERTS = 32
TOP_K = 4
D_FF = D_MODEL
SWIGLU_LIMIT = 7.0
SWIGLU_ALPHA = 1.702
EXPERT_BLOCK = 128
NORM_EPS = 1e-6
RWKV_IN = 3 * RWKV_DIM + RWKV_W_LORA + RWKV_A_LORA + RWKV_G_LORA
FOX_IN = 3 * FOX_DIM + FOX_HEADS
P_IN = RWKV_IN + POOL_DIM + FOX_IN

kernel_name = "hybrid_rwkv7_pool_fox_moe_adaln"


def _f32(t):
    return t.astype(jnp.float32)


def rms_norm(x, g):
    xf = _f32(x)
    y = xf * lax.rsqrt(jnp.mean(xf * xf, axis=-1, keepdims=True) + NORM_EPS)
    return (y * _f32(g)).astype(x.dtype)


def rwkv7_mixer(z, mu, w0, w_up, a0, a_up, g_up, k_k, k_a, r_k, ln_w, ln_b):
    B, S, _ = z.shape
    H, N = RWKV_HEADS, HEAD_DIM
    zf = _f32(z)
    prev = jnp.pad(zf, ((0, 0), (1, 0), (0, 0)))[:, :S]
    zf = zf + (prev - zf) * _f32(mu)
    cuts = [RWKV_DIM, 2 * RWKV_DIM, 3 * RWKV_DIM, 3 * RWKV_DIM + RWKV_W_LORA,
            3 * RWKV_DIM + RWKV_W_LORA + RWKV_A_LORA]
    r, k, v, wd, ad, gd = jnp.split(zf, cuts, axis=-1)
    w_log = -jax.nn.softplus(-(_f32(w0) + jnp.tanh(wd) @ _f32(w_up))) - 0.5
    decay = jnp.exp(-jnp.exp(w_log))
    a = jax.nn.sigmoid(_f32(a0) + ad @ _f32(a_up))
    g = jax.nn.sigmoid(gd) @ _f32(g_up)
    heads = lambda t: t.reshape(B, S, H, N)
    kk = heads(k * _f32(k_k))
    kk = kk / jnp.maximum(jnp.sqrt(jnp.sum(kk * kk, axis=-1, keepdims=True)), 1e-12)
    k = k * (1.0 + (a - 1.0) * _f32(k_a))
    r_h, k_h, v_h, w_h, a_h = heads(r), heads(k), heads(v), heads(decay), heads(a)

    def step(state, inp):
        r_t, w_t, k_t, v_t, kk_t, a_t = inp
        sa = jnp.einsum('bhvk,bhk->bhv', state, -kk_t)
        state = (state * w_t[:, :, None, :]
                 + sa[..., None] * (kk_t * a_t)[:, :, None, :]
                 + v_t[..., None] * k_t[:, :, None, :])
        return state, jnp.einsum('bhvk,bhk->bhv', state, r_t)

    xs = tuple(jnp.moveaxis(t, 1, 0) for t in (r_h, w_h, k_h, v_h, kk, a_h))
    s0 = jnp.zeros((B, H, N, N), jnp.float32)
    _, y = lax.scan(step, s0, xs)
    y = jnp.moveaxis(y, 0, 1)
    mean = jnp.mean(y, axis=-1, keepdims=True)
    var = jnp.mean(jnp.square(y - mean), axis=-1, keepdims=True)
    y = (y - mean) * lax.rsqrt(var + RWKV_GN_EPS) * _f32(ln_w).reshape(H, N) + _f32(ln_b).reshape(H, N)
    y = y + jnp.sum(r_h * k_h * _f32(r_k), axis=-1, keepdims=True) * v_h
    return (y.reshape(B, S, RWKV_DIM) * g).astype(z.dtype)


def pool_mixer(z, w_pool, scale):
    B, S, _ = z.shape
    zg = _f32(z).reshape(B, S, len(POOL_WINDOWS), POOL_GROUP)
    cs = jnp.cumsum(zg, axis=1)
    count = jnp.arange(1, S + 1, dtype=jnp.float32)
    outs = []
    for gi, w in enumerate(POOL_WINDOWS):
        c_g = cs[:, :, gi]
        lagged = jnp.pad(c_g, ((0, 0), (w, 0), (0, 0)))[:, :S]
        mean = (c_g - lagged) / jnp.minimum(count, w)[None, :, None]
        outs.append(mean - zg[:, :, gi])
    pooled = jnp.stack(outs, axis=2)
    y = jnp.einsum('bsgc,gcd->bsgd', pooled, _f32(w_pool)).reshape(B, S, POOL_DIM) * _f32(scale)
    return y.astype(z.dtype)


def fox_mixer(z, b_f):
    B, S, _ = z.shape
    H, Dh = FOX_HEADS, HEAD_DIM
    to_heads = lambda t: t.reshape(B, S, H, Dh).transpose(0, 2, 1, 3)
    q = to_heads(z[..., :FOX_DIM])
    k = to_heads(z[..., FOX_DIM:2 * FOX_DIM])
    v = to_heads(z[..., 2 * FOX_DIM:3 * FOX_DIM])
    log_f = jax.nn.log_sigmoid(_f32(z[..., 3 * FOX_DIM:]) + _f32(b_f))
    cum = jnp.cumsum(log_f, axis=1).transpose(0, 2, 1)
    nq = S // QUERY_BLOCK
    q_blocks = q.reshape(B, H, nq, QUERY_BLOCK, Dh).transpose(2, 0, 1, 3, 4)
    c_blocks = cum.reshape(B, H, nq, QUERY_BLOCK).transpose(2, 0, 1, 3)
    key_pos = jnp.arange(S)
    sm_scale = HEAD_DIM ** -0.5

    def block(args):
        q_blk, c_blk, i = args
        s = (_f32(jnp.einsum('bhqd,bhkd->bhqk', q_blk, k)) * sm_scale
             + (c_blk[..., None] - cum[:, :, None, :]))
        q_pos = i * QUERY_BLOCK + jnp.arange(QUERY_BLOCK)
        s = jnp.where(key_pos[None, :] <= q_pos[:, None], s, -jnp.inf)
        p = jax.nn.softmax(s, axis=-1)
        return jnp.einsum('bhqk,bhkd->bhqd', p.astype(v.dtype), v)

    o = lax.map(block, (q_blocks, c_blocks, jnp.arange(nq)))
    return o.transpose(1, 0, 3, 2, 4).reshape(B, S, FOX_DIM).astype(z.dtype)


def moe_ffn(h, router_w, router_b, w_gu, b_gu, w_down, b_down):
    B, S, D = h.shape
    N = B * S
    NK = N * TOP_K
    xf = h.reshape(N, D)
    logits = _f32(xf @ router_w + router_b)
    top_v, top_i = lax.top_k(logits, TOP_K)
    gates = jax.nn.softmax(top_v, axis=-1)
    e_flat = top_i.reshape(NK).astype(jnp.int32)
    g_flat = gates.reshape(NK)
    order = jnp.argsort(e_flat)
    se = e_flat[order]
    stok = (order // TOP_K).astype(jnp.int32)
    sg = g_flat[order]
    counts = jnp.bincount(e_flat, length=N_EXPERTS).astype(jnp.int32)
    padded = (counts + EXPERT_BLOCK - 1) // EXPERT_BLOCK * EXPERT_BLOCK
    pend = jnp.cumsum(padded)
    pstart = pend - padded
    gstart = jnp.cumsum(counts) - counts
    dest = pstart[se] + jnp.arange(NK, dtype=jnp.int32) - gstart[se]
    nb = -(-(NK + N_EXPERTS * (EXPERT_BLOCK - 1)) // EXPERT_BLOCK)
    rows = nb * EXPERT_BLOCK
    row_tok = jnp.zeros((rows,), jnp.int32).at[dest].set(stok)
    block_e = jnp.minimum(
        jnp.searchsorted(pend, jnp.arange(nb, dtype=jnp.int32) * EXPERT_BLOCK, side='right'),
        N_EXPERTS - 1).astype(jnp.int32)
    xb = xf[row_tok].reshape(nb, EXPERT_BLOCK, D)

    def expert_block(args):
        x_blk, e = args
        gu = x_blk @ w_gu[e] + b_gu[e]
        gate = jnp.minimum(gu[:, :D_FF], SWIGLU_LIMIT)
        up = jnp.clip(gu[:, D_FF:], -SWIGLU_LIMIT, SWIGLU_LIMIT)
        act = (up + 1.0) * (gate * jax.nn.sigmoid(SWIGLU_ALPHA * gate))
        return act @ w_down[e] + b_down[e]

    yb = lax.map(expert_block, (xb, block_e)).reshape(rows, D)
    y = jnp.zeros((N, D), h.dtype).at[stok].add(yb[dest] * sg[:, None].astype(h.dtype))
    return y.reshape(B, S, D)


def setup_inputs(seed: int = 0) -> dict:
    key = jax.random.key(seed)
    ks = iter(jax.random.split(key, 40))
    nrm = lambda shape, s: jax.random.normal(next(ks), shape, jnp.float32) * s
    L, D, E = DEPTH, D_MODEL, N_EXPERTS
    return {
        "x": nrm((BATCH, SEQ, D), 1.0),
        "c": nrm((BATCH, D), 1.0),
        "ada_w": nrm((L, D, 6 * D), 0.5 * D ** -0.5),
        "ada_b": nrm((L, 6 * D), 0.02),
        "norm1": 1.0 + nrm((L, D), 0.05),
        "w_in": nrm((L, D, P_IN), D ** -0.5),
        "w_out": nrm((L, D_MIX, D), D_MIX ** -0.5),
        "rwkv_mu": 0.5 + nrm((L, RWKV_IN), 0.1),
        "rwkv_w0": -2.5 + nrm((L, RWKV_DIM), 1.0),
        "rwkv_w_up": nrm((L, RWKV_W_LORA, RWKV_DIM), 0.5 * RWKV_W_LORA ** -0.5),
        "rwkv_a0": nrm((L, RWKV_DIM), 0.1),
        "rwkv_a_up": nrm((L, RWKV_A_LORA, RWKV_DIM), 0.5 * RWKV_A_LORA ** -0.5),
        "rwkv_g_up": nrm((L, RWKV_G_LORA, RWKV_DIM), RWKV_G_LORA ** -0.5),
        "rwkv_k_k": 0.85 + nrm((L, RWKV_DIM), 0.05),
        "rwkv_k_a": 1.0 + nrm((L, RWKV_DIM), 0.05),
        "rwkv_r_k": nrm((L, RWKV_HEADS, HEAD_DIM), 0.1),
        "rwkv_ln_w": 1.0 + nrm((L, RWKV_DIM), 0.05),
        "rwkv_ln_b": nrm((L, RWKV_DIM), 0.02),
        "pool_w": nrm((L, len(POOL_WINDOWS), POOL_GROUP, POOL_GROUP), POOL_GROUP ** -0.5),
        "pool_scale": 1.0 + nrm((L, POOL_DIM), 0.1),
        "fox_b_f": 2.0 + nrm((L, FOX_HEADS), 1.0),
        "norm2": 1.0 + nrm((L, D), 0.05),
        "router_w": nrm((L, D, E), D ** -0.5),
        "router_b": nrm((L, E), 0.01),
        "moe_w_gu": nrm((L, E, D, 2 * D_FF), D ** -0.5),
        "moe_b_gu": nrm((L, E, 2 * D_FF), 0.01),
        "moe_w_down": nrm((L, E, D_FF, D), D_FF ** -0.5),
        "moe_b_down": nrm((L, E, D), 0.01),
        "norm_f": 1.0 + nrm((D,), 0.05),
    }


def reference(x, c, ada_w, ada_b, norm1, w_in, w_out, rwkv_mu, rwkv_w0, rwkv_w_up, rwkv_a0,
              rwkv_a_up, rwkv_g_up, rwkv_k_k, rwkv_k_a, rwkv_r_k, rwkv_ln_w, rwkv_ln_b,
              pool_w, pool_scale, fox_b_f, norm2, router_w, router_b, moe_w_gu, moe_b_gu,
              moe_w_down, moe_b_down, norm_f):
    cond = jax.nn.silu(c)
    for l in range(DEPTH):
        mod = (cond @ ada_w[l] + ada_b[l])[:, None, :]
        sh1, sc1, g1, sh2, sc2, g2 = jnp.split(mod, 6, axis=-1)
        h = rms_norm(x, norm1[l]) * (1.0 + sc1) + sh1
        zp = h @ w_in[l]
        z_rwkv = zp[..., :RWKV_IN]
        z_pool = zp[..., RWKV_IN:RWKV_IN + POOL_DIM]
        z_fox = zp[..., RWKV_IN + POOL_DIM:]
        y_rwkv = rwkv7_mixer(z_rwkv, rwkv_mu[l], rwkv_w0[l], rwkv_w_up[l], rwkv_a0[l], rwkv_a_up[l],
                             rwkv_g_up[l], rwkv_k_k[l], rwkv_k_a[l], rwkv_r_k[l], rwkv_ln_w[l],
                             rwkv_ln_b[l])
        y_pool = pool_mixer(z_pool, pool_w[l], pool_scale[l])
        y_fox = fox_mixer(z_fox, fox_b_f[l])
        mixed = jnp.concatenate([y_rwkv, y_pool, y_fox], axis=-1)
        x = x + g1 * (mixed @ w_out[l])
        h = rms_norm(x, norm2[l]) * (1.0 + sc2) + sh2
        x = x + g2 * moe_ffn(h, router_w[l], router_b[l], moe_w_gu[l], moe_b_gu[l],
                             moe_w_down[l], moe_b_down[l])
    return rms_norm(x, norm_f)
```

```python
import functools

import jax
import jax.numpy as jnp
from jax import lax
from jax.experimental import pallas as pl
from jax.experimental.pallas import tpu as pltpu

F32 = jnp.float32
BF16 = jnp.bfloat16

D_MODEL = 1024
HEAD_DIM = 64
RWKV_DIM = 256
RWKV_HEADS = 4
RWKV_LORA_IN = 256
RWKV_GN_EPS = 64e-5
POOL_DIM = 256
POOL_WINDOWS = (2, 4, 8, 16)
POOL_HALO = 16
FOX_DIM = 512
FOX_HEADS = 8
N_EXPERTS = 32
TOP_K = 4
D_FF = 1024
SWIGLU_LIMIT = 7.0
SWIGLU_ALPHA = 1.702
NORM_EPS = 1e-6
RWKV_IN = 1024
MAIN_COLS = RWKV_IN + POOL_DIM + 3 * FOX_DIM
LANES = 128
RWKV_CHUNK = 64
NEG_BIG = -1e30
VMEM_LIMIT = 48 * 1024 * 1024

NN = (((1,), (0,)), ((), ()))
NT = (((1,), (1,)), ((), ()))
BNN = (((2,), (1,)), ((0,), (0,)))
BNT = (((2,), (2,)), ((0,), (0,)))


def _dg(a, b, dims):
    return lax.dot_general(a, b, dims, preferred_element_type=F32)


def _split2(x):
    hi = x.astype(BF16)
    lo = (x - hi.astype(F32)).astype(BF16)
    return hi, lo


def _split3(x):
    hi = x.astype(BF16)
    r = x - hi.astype(F32)
    mid = r.astype(BF16)
    lo = (r - mid.astype(F32)).astype(BF16)
    return hi, mid, lo


def _mm(a, b, dims, passes=1):
    if passes == 1:
        return _dg(a.astype(BF16), b.astype(BF16), dims)
    ah, al = _split2(a)
    bh, bl = _split2(b)
    return _dg(ah, bh, dims) + (_dg(ah, bl, dims) + _dg(al, bh, dims))


def _mm_exact_lhs(a_bf16, b, dims):
    hi, mid, lo = _split3(b)
    return _dg(a_bf16, hi, dims) + (_dg(a_bf16, mid, dims) + _dg(a_bf16, lo, dims))


def _mm_exact_rhs(a, b_bf16, dims):
    hi, mid, lo = _split3(a)
    return _dg(hi, b_bf16, dims) + (_dg(mid, b_bf16, dims) + _dg(lo, b_bf16, dims))


def _sigmoid(x):
    return 1.0 / (1.0 + jnp.exp(-x))


def _softplus(x):
    return jnp.maximum(x, 0.0) + jnp.log1p(jnp.exp(-jnp.abs(x)))


def _params(*sem):
    return pltpu.CompilerParams(dimension_semantics=sem, vmem_limit_bytes=VMEM_LIMIT)


def _ada_kernel(c_ref, w_ref, b_ref, o_ref):
    c = c_ref[...]
    cond = c * _sigmoid(c)
    o_ref[0] = _mm(cond, w_ref[0], NN, passes=3) + b_ref[0]


def ada_modulation(c, ada_w, ada_b):
    L, D, D6 = ada_w.shape
    B = c.shape[0]
    rows = 8
    cp = jnp.zeros((rows, D), F32).at[:B].set(c)
    out = pl.pallas_call(
        _ada_kernel,
        grid=(L, D6 // D),
        in_specs=[
            pl.BlockSpec((rows, D), lambda l, j: (0, 0)),
            pl.BlockSpec((1, D, D), lambda l, j: (l, 0, j)),
            pl.BlockSpec((1, 1, D), lambda l, j: (l, 0, j)),
        ],
        out_specs=pl.BlockSpec((1, rows, D), lambda l, j: (l, 0, j)),
        out_shape=jax.ShapeDtypeStruct((L, rows, D6), F32),
        compiler_params=_params("arbitrary", "arbitrary"),
        name="ada_modulation",
    )(cp, ada_w, ada_b.reshape(L, 1, D6))
    return out[:, :B].reshape(L, B, D6 // D, D)


def _rms_modulate(x, gain, scale, shift):
    ms = jnp.mean(x * x, axis=-1, keepdims=True)
    return x * lax.rsqrt(ms + NORM_EPS) * gain * (1.0 + scale) + shift


def _inproj_kernel(x_ref, mod_ref, n1_ref, wm_ref, wf_ref, bf_ref,
                   zr_ref, zp_ref, q_ref, k_ref, v_ref, cumc_ref, cumr_ref, carry_ref,
                   *, tiles_per_batch, tm):
    i = pl.program_id(0)

    @pl.when(i % tiles_per_batch == 0)
    def _():
        carry_ref[...] = jnp.zeros_like(carry_ref)

    mod = mod_ref[0]
    h = _rms_modulate(x_ref[...], n1_ref[...], mod[1:2], mod[0:1])
    z = _dg(h.astype(BF16), wm_ref[...], NN)
    zr_ref[...] = z[:, :RWKV_IN]
    zp_ref[...] = z[:, RWKV_IN:RWKV_IN + POOL_DIM]
    o = RWKV_IN + POOL_DIM
    q_ref[...] = (z[:, o:o + FOX_DIM] * (HEAD_DIM ** -0.5)).astype(BF16)
    k_ref[...] = z[:, o + FOX_DIM:o + 2 * FOX_DIM].astype(BF16)
    v_ref[...] = z[:, o + 2 * FOX_DIM:o + 3 * FOX_DIM].astype(BF16)

    fl = _mm(h, wf_ref[...], NN, passes=3) + bf_ref[...]
    logf = jnp.minimum(fl, 0.0) - jnp.log1p(jnp.exp(-jnp.abs(fl)))
    row = lax.broadcasted_iota(jnp.int32, (tm, tm), 0)
    col = lax.broadcasted_iota(jnp.int32, (tm, tm), 1)
    tri = (row >= col).astype(BF16)
    cum = _mm_exact_lhs(tri, logf, NN) + carry_ref[...]
    carry_ref[...] = cum[tm - 1:tm, :]
    cumc_ref[...] = cum
    sel = (lax.broadcasted_iota(jnp.int32, (8, LANES), 0)
           == lax.broadcasted_iota(jnp.int32, (8, LANES), 1)).astype(BF16)
    cumr_ref[0, 0] = _mm_exact_lhs(sel, cum, NT)


def in_projection(x2, mod, norm1, w_main, w_f, b_f, *, batch, seq, tm):
    n, d = x2.shape
    tpb = seq // tm
    row = lambda i: (i, 0)
    const = lambda i: (0, 0)
    out_shapes = (
        jax.ShapeDtypeStruct((n, RWKV_IN), F32),
        jax.ShapeDtypeStruct((n, POOL_DIM), F32),
        jax.ShapeDtypeStruct((n, FOX_DIM), BF16),
        jax.ShapeDtypeStruct((n, FOX_DIM), BF16),
        jax.ShapeDtypeStruct((n, FOX_DIM), BF16),
        jax.ShapeDtypeStruct((n, LANES), F32),
        jax.ShapeDtypeStruct((batch, tpb, 8, tm), F32),
    )
    return pl.pallas_call(
        functools.partial(_inproj_kernel, tiles_per_batch=tpb, tm=tm),
        grid=(n // tm,),
        in_specs=[
            pl.BlockSpec((tm, d), row),
            pl.BlockSpec((1, 6, d), lambda i: (i // tpb, 0, 0)),
            pl.BlockSpec((1, d), const),
            pl.BlockSpec((d, MAIN_COLS), const),
            pl.BlockSpec((d, LANES), const),
            pl.BlockSpec((1, LANES), const),
        ],
        out_specs=(
            pl.BlockSpec((tm, RWKV_IN), row),
            pl.BlockSpec((tm, POOL_DIM), row),
            pl.BlockSpec((tm, FOX_DIM), row),
            pl.BlockSpec((tm, FOX_DIM), row),
            pl.BlockSpec((tm, FOX_DIM), row),
            pl.BlockSpec((tm, LANES), row),
            pl.BlockSpec((1, 1, 8, tm), lambda i: (i // tpb, i % tpb, 0, 0)),
        ),
        out_shape=out_shapes,
        scratch_shapes=[pltpu.VMEM((1, LANES), F32)],
        compiler_params=_params("arbitrary"),
        name="in_projection",
    )(x2, mod, norm1, w_main, w_f, b_f)


def _unit_lower_inverse(a, passes):
    c = a.shape[-1]
    ri = lax.broadcasted_iota(jnp.int32, (c, c), 0)
    ci = lax.broadcasted_iota(jnp.int32, (c, c), 1)
    same16 = (ri // 16) == (ci // 16)
    same32 = (ri // 32) == (ci // 32)
    eye = (ri == ci).astype(F32)
    mm = lambda u, v: _mm(u, v, BNN, passes)
    a16 = jnp.where(same16, a, 0.0)
    t = eye + a16
    p = a16
    for _ in range(3):
        p = mm(p, p)
        t = t + mm(t, p)
    o16 = jnp.where(same32 & jnp.logical_not(same16), a, 0.0)
    t = t + mm(t, mm(o16, t))
    o32 = jnp.where(same32, 0.0, a)
    return t + mm(t, mm(o32, t))


def _rwkv_kernel(z_ref, mu_ref, pv_ref, wl_ref, y_ref, prev_ref, st_ref, *, g, passes):
    c = RWKV_CHUNK
    kd = RWKV_DIM
    nh = RWKV_HEADS
    p_score, p_inv, p_apply, p_state = passes

    @pl.when(pl.program_id(0) == 0)
    def _():
        prev_ref[...] = jnp.zeros_like(prev_ref)
        st_ref[...] = jnp.zeros_like(st_ref)

    mu = mu_ref[...]
    row0 = lax.broadcasted_iota(jnp.int32, (c, RWKV_IN), 0) == 0
    zs = []
    for b in range(g):
        zb = z_ref[b]
        prev = jnp.where(row0, prev_ref[b], pltpu.roll(zb, 1, 0))
        prev_ref[b] = zb[c - 1:c, :]
        zs.append(zb + (prev - zb) * mu)
    zf = jnp.stack(zs, axis=0).reshape(g * c, RWKV_IN)

    pv = pv_ref[...]
    w0, a0, k_k, k_a, r_k, ln_w, ln_b = (pv[i:i + 1] for i in range(7))
    r = zf[:, 0:kd]
    k = zf[:, kd:2 * kd]
    v = zf[:, 2 * kd:3 * kd]
    lo = zf[:, 3 * kd:]
    lane = lax.broadcasted_iota(jnp.int32, (1, kd), 1)
    lora_in = jnp.where(lane < 64, jnp.tanh(lo), jnp.where(lane < 128, lo, _sigmoid(lo)))
    lin = _dg(lora_in.astype(BF16), wl_ref[...], NN)
    lw = -jnp.exp(-_softplus(-(w0 + lin[:, 0:kd])) - 0.5)
    a = _sigmoid(a0 + lin[:, kd:2 * kd])
    gate = lin[:, 2 * kd:]

    hrow = lax.broadcasted_iota(jnp.int32, (kd, kd), 0) // HEAD_DIM
    hcol = lax.broadcasted_iota(jnp.int32, (kd, kd), 1) // HEAD_DIM
    same_head = hrow == hcol
    ones_bd = same_head.astype(BF16)
    head_sum = lambda t: _mm_exact_rhs(t, ones_bd, NN)

    kkr = k * k_k
    kk = kkr / jnp.maximum(jnp.sqrt(head_sum(kkr * kkr)), 1e-12)
    kp = k * (1.0 + (a - 1.0) * k_a)
    bonus = head_sum(r * kp * r_k) * v

    to3 = lambda t: t.reshape(g, c, kd)
    r3, kp3, v3, kk3, a3, lw3 = (to3(t) for t in (r, kp, v, kk, a, lw))

    ri = lax.broadcasted_iota(jnp.int32, (c, c), 0)
    ci = lax.broadcasted_iota(jnp.int32, (c, c), 1)
    strict = ri > ci
    incl = ri >= ci
    tri = jnp.broadcast_to(incl.astype(BF16)[None], (g, c, c))
    cum = _mm_exact_lhs(tri, lw3, BNN)
    cum_end = cum[:, c - 1:c, :]
    e_in = jnp.exp(cum)
    e_out = jnp.exp(-cum)
    e_tail = jnp.exp(cum_end - cum)
    p_end = jnp.exp(cum_end)
    beta = kk3 * a3
    ab = -kk3 * jnp.exp(cum - lw3)
    rb = r3 * e_in
    bb = beta * e_out
    kb = kp3 * e_out
    bt = beta * e_tail
    kt = kp3 * e_tail

    hmask = [(lane // HEAD_DIM == h).astype(F32) for h in range(nh)]
    rep = lambda t: jnp.concatenate([t] * nh, axis=0)
    heads = lambda t: jnp.concatenate([t * hmask[h] for h in range(nh)], axis=0)

    def merge(t):
        out = t[0:g] * hmask[0]
        for h in range(1, nh):
            out = out + t[h * g:(h + 1) * g] * hmask[h]
        return out

    lhs = heads(jnp.concatenate([ab, rb], axis=1))
    sb = _mm(lhs, rep(bb), BNT, p_score)
    sk = _mm(lhs, rep(kb), BNT, p_score)
    a_ab = jnp.where(strict, sb[:, :c, :], 0.0)
    a_rb = jnp.where(incl, sb[:, c:, :], 0.0)
    a_ak = jnp.where(strict, sk[:, :c, :], 0.0)
    a_rk = jnp.where(incl, sk[:, c:, :], 0.0)
    t_inv = _unit_lower_inverse(a_ab, p_inv)

    app = lambda u, w: _mm(u, w, BNN, p_apply)
    v_rep = rep(v3)
    akv = app(a_ak, v_rep)
    w_h = app(t_inv, rep(ab))
    u0_h = app(t_inv, akv)
    w_all = merge(w_h)
    u0_all = merge(u0_h)
    rq = rb + merge(app(a_rb, w_h))
    yc = merge(app(a_rb, u0_h) + app(a_rk, v_rep))

    bt_t = jnp.swapaxes(bt, 1, 2)
    kt_t = jnp.swapaxes(kt, 1, 2)
    eye = (lax.broadcasted_iota(jnp.int32, (kd, kd), 0)
           == lax.broadcasted_iota(jnp.int32, (kd, kd), 1))
    trans = jnp.where(same_head, app(bt_t, w_all), 0.0) + jnp.where(eye, p_end, 0.0)
    inject = jnp.where(same_head, app(bt_t, u0_all) + app(kt_t, v3), 0.0)

    st = st_ref[...]
    y = _mm(rq, st, BNN, p_state) + yc
    st_ref[...] = _mm(trans, st, BNN, p_state) + inject

    y2 = y.reshape(g * c, kd)
    mean = head_sum(y2) * (1.0 / HEAD_DIM)
    dv = y2 - mean
    var = head_sum(dv * dv) * (1.0 / HEAD_DIM)
    out = (dv * lax.rsqrt(var + RWKV_GN_EPS) * ln_w + ln_b + bonus) * gate
    y_ref[...] = out.reshape(g, c, kd).astype(y_ref.dtype)


def rwkv_mixer(z3, mu, pvec, w_lora, *, passes=(3, 3, 3, 3)):
    b, s, _ = z3.shape
    c = RWKV_CHUNK
    return pl.pallas_call(
        functools.partial(_rwkv_kernel, g=b, passes=passes),
        grid=(s // c,),
        in_specs=[
            pl.BlockSpec((b, c, RWKV_IN), lambda t: (0, t, 0)),
            pl.BlockSpec((1, RWKV_IN), lambda t: (0, 0)),
            pl.BlockSpec((8, RWKV_DIM), lambda t: (0, 0)),
            pl.BlockSpec((RWKV_LORA_IN, 3 * RWKV_DIM), lambda t: (0, 0)),
        ],
        out_specs=pl.BlockSpec((b, c, RWKV_DIM), lambda t: (0, t, 0)),
        out_shape=jax.ShapeDtypeStruct((b, s, RWKV_DIM), BF16),
        scratch_shapes=[
            pltpu.VMEM((b, 1, RWKV_IN), F32),
            pltpu.VMEM((b, RWKV_DIM, RWKV_DIM), F32),
        ],
        compiler_params=_params("arbitrary"),
        name="rwkv_mixer",
    )(z3, mu, pvec, w_lora)


def _pool_kernel(z_ref, w_ref, sc_ref, y_ref, ext_ref, *, tp):
    t = pl.program_id(1)
    hl = POOL_HALO
    levels = len(POOL_WINDOWS)

    @pl.when(t == 0)
    def _():
        ext_ref[:, 0:hl, :] = jnp.zeros((levels, hl, POOL_DIM), F32)

    z = z_ref[0]
    sums = []
    cur = z
    for lvl, w in enumerate(POOL_WINDOWS):
        ext_ref[lvl, hl:hl + tp, :] = cur
        back = w // 2
        cur = cur + ext_ref[lvl, hl - back:hl - back + tp, :]
        sums.append(cur)
    for lvl in range(levels):
        ext_ref[lvl, 0:hl, :] = ext_ref[lvl, tp:tp + hl, :]

    lane = lax.broadcasted_iota(jnp.int32, (tp, POOL_DIM), 1)
    pos = (t * tp + lax.broadcasted_iota(jnp.int32, (tp, POOL_DIM), 0) + 1).astype(F32)
    grp = POOL_DIM // levels
    win_sum = sums[-1]
    count = jnp.minimum(pos, float(POOL_WINDOWS[-1]))
    for lvl in range(levels - 2, -1, -1):
        in_grp = lane < (lvl + 1) * grp
        win_sum = jnp.where(in_grp, sums[lvl], win_sum)
        count = jnp.where(in_grp, jnp.minimum(pos, float(POOL_WINDOWS[lvl])), count)
    pooled = win_sum / count - z
    y = _dg(pooled.astype(BF16), w_ref[...], NN) * sc_ref[...]
    y_ref[0] = y.astype(y_ref.dtype)


def pool_mixer(zp3, w_bd, scale, *, tp):
    b, s, _ = zp3.shape
    return pl.pallas_call(
        functools.partial(_pool_kernel, tp=tp),
        grid=(b, s // tp),
        in_specs=[
            pl.BlockSpec((1, tp, POOL_DIM), lambda i, t: (i, t, 0)),
            pl.BlockSpec((POOL_DIM, POOL_DIM), lambda i, t: (0, 0)),
            pl.BlockSpec((1, POOL_DIM), lambda i, t: (0, 0)),
        ],
        out_specs=pl.BlockSpec((1, tp, POOL_DIM), lambda i, t: (i, t, 0)),
        out_shape=jax.ShapeDtypeStruct((b, s, POOL_DIM), BF16),
        scratch_shapes=[pltpu.VMEM((len(POOL_WINDOWS), POOL_HALO + tp, POOL_DIM), F32)],
        compiler_params=_params("arbitrary", "arbitrary"),
        name="pool_mixer",
    )(zp3, w_bd, scale)


def _fox_kernel(q_ref, k_ref, v_ref, cq_ref, ck_ref, o_ref, m_ref, l_ref, acc_ref, *, tq):
    hp = pl.program_id(1)
    i = pl.program_id(2)
    lane = lax.broadcasted_iota(jnp.int32, (1, LANES), 1)
    q2 = q_ref[0]
    cq = cq_ref[0]
    m_ref[...] = jnp.full(m_ref.shape, NEG_BIG, F32)
    l_ref[...] = jnp.zeros(l_ref.shape, F32)
    acc_ref[...] = jnp.zeros(acc_ref.shape, F32)
    qh = [jnp.where(lane // HEAD_DIM == hh, q2, jnp.zeros_like(q2)) for hh in range(2)]
    cqh = [jnp.sum(jnp.where(lane == 2 * hp + hh, cq, 0.0), axis=-1, keepdims=True) for hh in range(2)]

    def step(j, masked):
        start = pl.multiple_of(j * tq, tq)
        k2 = k_ref[0, pl.ds(start, tq), :]
        v2 = v_ref[0, pl.ds(start, tq), :]
        for hh in range(2):
            s = _dg(qh[hh], k2, NT)
            ck = ck_ref[0, j, pl.ds(2 * hp + hh, 1), :]
            s = s + (cqh[hh] - ck)
            if masked:
                qpos = lax.broadcasted_iota(jnp.int32, (tq, tq), 0)
                kpos = lax.broadcasted_iota(jnp.int32, (tq, tq), 1)
                s = jnp.where(kpos <= qpos, s, NEG_BIG)
            m_old = m_ref[hh][:, 0:1]
            m_new = jnp.maximum(m_old, jnp.max(s, axis=-1, keepdims=True))
            alpha = jnp.exp(m_old - m_new)
            p = jnp.exp(s - m_new)
            l_ref[hh] = alpha * l_ref[hh] + jnp.sum(p, axis=-1, keepdims=True)
            acc_ref[hh] = alpha * acc_ref[hh] + _dg(p.astype(BF16), v2, NN)
            m_ref[hh] = jnp.broadcast_to(m_new, (tq, LANES))

    def body(j, carry):
        step(j, False)
        return carry

    lax.fori_loop(0, i, body, 0)
    step(i, True)
    o0 = acc_ref[0] / l_ref[0]
    o1 = acc_ref[1] / l_ref[1]
    o_ref[0] = jnp.where(lane < HEAD_DIM, o0, o1).astype(o_ref.dtype)


def fox_attention(q3, k3, v3, cumc3, cumr4, *, tq):
    b, s, _ = q3.shape
    nq = s // tq
    pairs = FOX_HEADS // 2
    return pl.pallas_call(
        functools.partial(_fox_kernel, tq=tq),
        grid=(b, pairs, nq),
        in_specs=[
            pl.BlockSpec((1, tq, LANES), lambda bi, hp, i: (bi, i, hp)),
            pl.BlockSpec((1, s, LANES), lambda bi, hp, i: (bi, 0, hp)),
            pl.BlockSpec((1, s, LANES), lambda bi, hp, i: (bi, 0, hp)),
            pl.BlockSpec((1, tq, LANES), lambda bi, hp, i: (bi, i, 0)),
            pl.BlockSpec((1, nq, 8, tq), lambda bi, hp, i: (bi, 0, 0, 0)),
        ],
        out_specs=pl.BlockSpec((1, tq, LANES), lambda bi, hp, i: (bi, i, hp)),
        out_shape=jax.ShapeDtypeStruct((b, s, FOX_DIM), BF16),
        scratch_shapes=[
            pltpu.VMEM((2, tq, LANES), F32),
            pltpu.VMEM((2, tq, LANES), F32),
            pltpu.VMEM((2, tq, LANES), F32),
        ],
        compiler_params=_params("arbitrary", "arbitrary", "arbitrary"),
        name="fox_attention",
    )(q3, k3, v3, cumc3, cumr4)


def _outproj_kernel(yr_ref, yp_ref, yf_ref, x_ref, mod_ref, wo_ref, n2_ref, rw_ref, rb_ref,
                    x1_ref, h2_ref, ti_ref, tg_ref):
    mod = mod_ref[0]
    mixed = (_dg(yr_ref[...], wo_ref[0:RWKV_DIM, :], NN)
             + _dg(yp_ref[...], wo_ref[RWKV_DIM:RWKV_DIM + POOL_DIM, :], NN)
             + _dg(yf_ref[...], wo_ref[RWKV_DIM + POOL_DIM:, :], NN))
    x1 = x_ref[...] + mod[2:3] * mixed
    x1_ref[...] = x1
    h2 = _rms_modulate(x1, n2_ref[...], mod[4:5], mod[3:4])
    h2_ref[...] = h2

    tm = x1.shape[0]
    lane = lax.broadcasted_iota(jnp.int32, (tm, LANES), 1)
    lane_f = lane.astype(F32)
    logits = _mm(h2, rw_ref[...], NN, passes=3) + rb_ref[...]
    work = jnp.where(lane < N_EXPERTS, logits, NEG_BIG)
    top_i = jnp.zeros((tm, LANES), F32)
    top_e = jnp.zeros((tm, LANES), F32)
    v0 = None
    for kk in range(TOP_K):
        vmax = jnp.max(work, axis=-1, keepdims=True)
        idx = jnp.min(jnp.where(work == vmax, lane_f, float(LANES)), axis=-1, keepdims=True)
        if kk == 0:
            v0 = vmax
        top_i = jnp.where(lane == kk, idx, top_i)
        top_e = jnp.where(lane == kk, jnp.exp(vmax - v0), top_e)
        work = jnp.where(lane_f == idx, NEG_BIG, work)
    ti_ref[...] = top_i.astype(jnp.int32)
    tg_ref[...] = top_e / jnp.sum(top_e, axis=-1, keepdims=True)


def out_projection(yr, yp, yf, x2, mod, w_out, norm2, rw, rb, *, seq, tm):
    n, d = x2.shape
    tpb = seq // tm
    row = lambda i: (i, 0)
    const = lambda i: (0, 0)
    return pl.pallas_call(
        _outproj_kernel,
        grid=(n // tm,),
        in_specs=[
            pl.BlockSpec((tm, RWKV_DIM), row),
            pl.BlockSpec((tm, POOL_DIM), row),
            pl.BlockSpec((tm, FOX_DIM), row),
            pl.BlockSpec((tm, d), row),
            pl.BlockSpec((1, 6, d), lambda i: (i // tpb, 0, 0)),
            pl.BlockSpec((d, d), const),
            pl.BlockSpec((1, d), const),
            pl.BlockSpec((d, LANES), const),
            pl.BlockSpec((1, LANES), const),
        ],
        out_specs=(
            pl.BlockSpec((tm, d), row),
            pl.BlockSpec((tm, d), row),
            pl.BlockSpec((tm, LANES), row),
            pl.BlockSpec((tm, LANES), row),
        ),
        out_shape=(
            jax.ShapeDtypeStruct((n, d), F32),
            jax.ShapeDtypeStruct((n, d), F32),
            jax.ShapeDtypeStruct((n, LANES), jnp.int32),
            jax.ShapeDtypeStruct((n, LANES), F32),
        ),
        compiler_params=_params("arbitrary"),
        name="out_projection",
    )(yr, yp, yf, x2, mod, w_out, norm2, rw, rb)


def _routing_tables(top_i, *, rows_per_block):
    n = top_i.shape[0]
    r = rows_per_block
    e_flat = top_i.reshape(n * TOP_K)
    onehot = (e_flat[:, None] == jnp.arange(N_EXPERTS, dtype=jnp.int32)[None, :]).astype(jnp.int32)
    csum = jnp.cumsum(onehot, axis=0)
    rank = jnp.sum(csum * onehot, axis=1) - 1
    counts = csum[-1]
    padded = (counts + r - 1) // r * r
    pend = jnp.cumsum(padded)
    pstart = pend - padded
    dest = (pstart[e_flat] + rank).astype(jnp.int32)
    nb = -(-(n * TOP_K + N_EXPERTS * (r - 1)) // r)
    bstart = jnp.arange(nb, dtype=jnp.int32) * r
    block_e = jnp.minimum(jnp.searchsorted(pend, bstart, side='right'), N_EXPERTS - 1).astype(jnp.int32)
    nvalid = jnp.clip(pstart[block_e] + counts[block_e] - bstart, 0, r).astype(jnp.int32)
    nused = (pend[-1] // r).astype(jnp.int32).reshape(1)
    return dest, block_e, nvalid, nused, nb


def _dispatch_kernel(dest_ref, h_ref, xs_ref, sem, *, td):
    base = pl.program_id(0) * td

    def issue(t, carry):
        for kk in range(TOP_K):
            pltpu.make_async_copy(h_ref.at[pl.ds(base + t, 1)],
                                  xs_ref.at[pl.ds(dest_ref[0, 0, TOP_K * t + kk], 1)], sem).start()
        return carry

    lax.fori_loop(0, td, issue, 0)

    def drain(t, carry):
        for kk in range(TOP_K):
            pltpu.make_async_copy(h_ref.at[pl.ds(0, 1)], xs_ref.at[pl.ds(0, 1)], sem).wait()
        return carry

    lax.fori_loop(0, td, drain, 0)


def moe_dispatch(h2, dest2, *, rows, td):
    n, d = h2.shape
    return pl.pallas_call(
        functools.partial(_dispatch_kernel, td=td),
        grid=(n // td,),
        in_specs=[
            pl.BlockSpec((1, 1, TOP_K * td), lambda i: (i, 0, 0), memory_space=pltpu.SMEM),
            pl.BlockSpec(memory_space=pl.ANY),
        ],
        out_specs=pl.BlockSpec(memory_space=pl.ANY),
        out_shape=jax.ShapeDtypeStruct((rows, d), F32),
        scratch_shapes=[pltpu.SemaphoreType.DMA],
        compiler_params=_params("arbitrary"),
        name="moe_dispatch",
    )(dest2, h2)


def _expert_kernel(be_ref, nv_ref, nu_ref, x_ref, wgu_ref, bgu_ref, wd_ref, bd_ref, y_ref,
                   wgu_bf, wd_bf, cur_ref, *, r):
    b = pl.program_id(0)
    e = be_ref[b]
    nv = nv_ref[b]

    @pl.when(b == 0)
    def _():
        cur_ref[0] = -1

    @pl.when((nv > 0) & (cur_ref[0] != e))
    def _():
        cur_ref[0] = e
        step = 128

        def cast(i, carry):
            s = pl.multiple_of(i * step, step)
            wgu_bf[pl.ds(s, step), :] = wgu_ref[0, pl.ds(s, step), :].astype(BF16)
            wd_bf[pl.ds(s, step), :] = wd_ref[0, pl.ds(s, step), :].astype(BF16)
            return carry

        lax.fori_loop(0, D_MODEL // step, cast, 0)

    @pl.when(nv > 0)
    def _():
        rows = lax.broadcasted_iota(jnp.int32, (r, 1), 0)
        x = jnp.where(rows < nv, x_ref[...], 0.0).astype(BF16)
        gu = _dg(x, wgu_bf[...], NN) + bgu_ref[0]
        gate = jnp.minimum(gu[:, :D_FF], SWIGLU_LIMIT)
        up = jnp.clip(gu[:, D_FF:], -SWIGLU_LIMIT, SWIGLU_LIMIT)
        act = (up + 1.0) * (gate * _sigmoid(SWIGLU_ALPHA * gate))
        y_ref[...] = _dg(act.astype(BF16), wd_bf[...], NN) + bd_ref[0]


def moe_experts(xs, block_e, nvalid, nused, w_gu, b_gu, w_down, b_down, *, r):
    rows, d = xs.shape
    nb = rows // r
    e = w_gu.shape[0]
    blk = lambda b, be, nv, nu: (jnp.minimum(b, nu[0] - 1), 0)
    wsel = lambda b, be, nv, nu: (be[b], 0, 0)
    grid_spec = pltpu.PrefetchScalarGridSpec(
        num_scalar_prefetch=3,
        grid=(nb,),
        in_specs=[
            pl.BlockSpec((r, d), blk),
            pl.BlockSpec((1, d, 2 * D_FF), wsel),
            pl.BlockSpec((1, 1, 2 * D_FF), wsel),
            pl.BlockSpec((1, D_FF, d), wsel),
            pl.BlockSpec((1, 1, d), wsel),
        ],
        out_specs=pl.BlockSpec((r, d), blk),
        scratch_shapes=[
            pltpu.VMEM((d, 2 * D_FF), BF16),
            pltpu.VMEM((D_FF, d), BF16),
            pltpu.SMEM((1,), jnp.int32),
        ],
    )
    return pl.pallas_call(
        functools.partial(_expert_kernel, r=r),
        grid_spec=grid_spec,
        out_shape=jax.ShapeDtypeStruct((rows, d), F32),
        compiler_params=_params("arbitrary"),
        name="moe_experts",
    )(block_e, nvalid, nused, xs, w_gu, b_gu.reshape(e, 1, 2 * D_FF), w_down, b_down.reshape(e, 1, d))


def _combine_kernel(dest_ref, yb_ref, tg_ref, x_ref, mod_ref, nf_ref, o_ref, buf, sem, *, tc, final_norm):
    def issue(t, carry):
        for kk in range(TOP_K):
            pltpu.make_async_copy(yb_ref.at[pl.ds(dest_ref[0, 0, TOP_K * t + kk], 1)],
                                  buf.at[kk, pl.ds(t, 1)], sem).start()
        return carry

    lax.fori_loop(0, tc, issue, 0)

    def drain(t, carry):
        for kk in range(TOP_K):
            pltpu.make_async_copy(yb_ref.at[pl.ds(0, 1)], buf.at[kk, pl.ds(t, 1)], sem).wait()
        return carry

    lax.fori_loop(0, tc, drain, 0)

    tg = tg_ref[...]
    y = tg[:, 0:1] * buf[0]
    for kk in range(1, TOP_K):
        y = y + tg[:, kk:kk + 1] * buf[kk]
    x = x_ref[...] + mod_ref[0][5:6] * y
    if final_norm:
        ms = jnp.mean(x * x, axis=-1, keepdims=True)
        x = x * lax.rsqrt(ms + NORM_EPS) * nf_ref[...]
    o_ref[...] = x


def moe_combine(yb, dest2, tg, x1, mod, norm_f, *, seq, tc, final_norm):
    n, d = x1.shape
    tpb = seq // tc
    row = lambda i: (i, 0)
    return pl.pallas_call(
        functools.partial(_combine_kernel, tc=tc, final_norm=final_norm),
        grid=(n // tc,),
        in_specs=[
            pl.BlockSpec((1, 1, TOP_K * tc), lambda i: (i, 0, 0), memory_space=pltpu.SMEM),
            pl.BlockSpec(memory_space=pl.ANY),
            pl.BlockSpec((tc, LANES), row),
            pl.BlockSpec((tc, d), row),
            pl.BlockSpec((1, 6, d), lambda i: (i // tpb, 0, 0)),
            pl.BlockSpec((1, d), lambda i: (0, 0)),
        ],
        out_specs=pl.BlockSpec((tc, d), row),
        out_shape=jax.ShapeDtypeStruct((n, d), F32),
        scratch_shapes=[pltpu.VMEM((TOP_K, tc, d), F32), pltpu.SemaphoreType.DMA],
        compiler_params=_params("arbitrary"),
        name="moe_combine",
    )(dest2, yb, tg, x1, mod, norm_f)


def _block_diag(blocks):
    g, m, _ = blocks.shape
    out = jnp.zeros((g * m, g * m), blocks.dtype)
    for i in range(g):
        out = out.at[i * m:(i + 1) * m, i * m:(i + 1) * m].set(blocks[i])
    return out


def _pad_cols(w, cols):
    return jnp.zeros(w.shape[:-1] + (cols,), w.dtype).at[..., :w.shape[-1]].set(w)


def trunk(x, c, ada_w, ada_b, norm1, w_in, w_out, rwkv_mu, rwkv_w0, rwkv_w_up, rwkv_a0, rwkv_a_up,
          rwkv_g_up, rwkv_k_k, rwkv_k_a, rwkv_r_k, rwkv_ln_w, rwkv_ln_b, pool_w, pool_scale, fox_b_f,
          norm2, router_w, router_b, moe_w_gu, moe_b_gu, moe_w_down, moe_b_down, norm_f,
          *, tm=512, tp=512, tq=512, expert_rows=256, td=1024, tc=256, rwkv_passes=(3, 3, 3, 3)):
    batch, seq, d = x.shape
    assert tm == tq, "the attention kernel reads the forget sums in the in-projection's time tiling"
    depth = ada_w.shape[0]
    n = batch * seq
    mod_all = ada_modulation(c, ada_w, ada_b)
    x2 = x.reshape(n, d)
    nf = norm_f.reshape(1, d)
    for l in range(depth):
        mod = mod_all[l]
        w_main = w_in[l][:, :MAIN_COLS].astype(BF16)
        w_f = _pad_cols(w_in[l][:, MAIN_COLS:], LANES)
        b_f = _pad_cols(fox_b_f[l].reshape(1, FOX_HEADS), LANES)
        w_lora = jnp.zeros((RWKV_LORA_IN, 3 * RWKV_DIM), F32)
        w_lora = w_lora.at[0:64, 0:RWKV_DIM].set(rwkv_w_up[l])
        w_lora = w_lora.at[64:128, RWKV_DIM:2 * RWKV_DIM].set(rwkv_a_up[l])
        w_lora = w_lora.at[128:256, 2 * RWKV_DIM:].set(rwkv_g_up[l]).astype(BF16)
        pvec = jnp.stack([rwkv_w0[l], rwkv_a0[l], rwkv_k_k[l], rwkv_k_a[l], rwkv_r_k[l].reshape(-1),
                          rwkv_ln_w[l], rwkv_ln_b[l], jnp.zeros((RWKV_DIM,), F32)], axis=0)
        w_pool = _block_diag(pool_w[l]).astype(BF16)
        rw = _pad_cols(router_w[l], LANES)
        rb = _pad_cols(router_b[l].reshape(1, N_EXPERTS), LANES)

        zr, zp, q, k, v, cumc, cumr = in_projection(
            x2, mod, norm1[l].reshape(1, d), w_main, w_f, b_f, batch=batch, seq=seq, tm=tm)
        yr = rwkv_mixer(zr.reshape(batch, seq, RWKV_IN), rwkv_mu[l].reshape(1, RWKV_IN), pvec, w_lora,
                        passes=rwkv_passes)
        yp = pool_mixer(zp.reshape(batch, seq, POOL_DIM), w_pool, pool_scale[l].reshape(1, POOL_DIM), tp=tp)
        to3 = lambda t: t.reshape(batch, seq, t.shape[-1])
        yf = fox_attention(to3(q), to3(k), to3(v), to3(cumc), cumr, tq=tq)
        x1, h2, top_i, top_g = out_projection(
            yr.reshape(n, RWKV_DIM), yp.reshape(n, POOL_DIM), yf.reshape(n, FOX_DIM), x2, mod,
            w_out[l].astype(BF16), norm2[l].reshape(1, d), rw, rb, seq=seq, tm=tm)

        dest, block_e, nvalid, nused, nb = _routing_tables(top_i[:, :TOP_K], rows_per_block=expert_rows)
        xs = moe_dispatch(h2, dest.reshape(n // td, 1, TOP_K * td), rows=nb * expert_rows, td=td)
        yb = moe_experts(xs, block_e, nvalid, nused, moe_w_gu[l], moe_b_gu[l], moe_w_down[l], moe_b_down[l],
                         r=expert_rows)
        x2 = moe_combine(yb, dest.reshape(n // tc, 1, TOP_K * tc), top_g, x1, mod, nf, seq=seq, tc=tc,
                         final_norm=(l == depth - 1))
    return x2.reshape(batch, seq, d)


def kernel(x, c, ada_w, ada_b, norm1, w_in, w_out, rwkv_mu, rwkv_w0, rwkv_w_up, rwkv_a0, rwkv_a_up, rwkv_g_up, rwkv_k_k, rwkv_k_a, rwkv_r_k, rwkv_ln_w, rwkv_ln_b, pool_w, pool_scale, fox_b_f, norm2, router_w, router_b, moe_w_gu, moe_b_gu, moe_w_down, moe_b_down, norm_f):
    return trunk(x, c, ada_w, ada_b, norm1, w_in, w_out, rwkv_mu, rwkv_w0, rwkv_w_up, rwkv_a0, rwkv_a_up,
                 rwkv_g_up, rwkv_k_k, rwkv_k_a, rwkv_r_k, rwkv_ln_w, rwkv_ln_b, pool_w, pool_scale, fox_b_f,
                 norm2, router_w, router_b, moe_w_gu, moe_b_gu, moe_w_down, moe_b_down, norm_f)
```

```python
import functools

import jax
import jax.numpy as jnp
from jax import lax
from jax.experimental import pallas as pl
from jax.experimental.pallas import tpu as pltpu

F32 = jnp.float32
BF16 = jnp.bfloat16

D_MODEL = 1024
HEAD_DIM = 64
RWKV_DIM = 256
RWKV_HEADS = 4
RWKV_LORA_IN = 256
RWKV_GN_EPS = 64e-5
POOL_DIM = 256
POOL_WINDOWS = (2, 4, 8, 16)
POOL_HALO = 16
FOX_DIM = 512
FOX_HEADS = 8
N_EXPERTS = 32
TOP_K = 4
D_FF = 1024
SWIGLU_LIMIT = 7.0
SWIGLU_ALPHA = 1.702
NORM_EPS = 1e-6
RWKV_IN = 1024
MAIN_COLS = RWKV_IN + POOL_DIM + 3 * FOX_DIM
LANES = 128
RWKV_CHUNK = 64
NEG_BIG = -1e30
VMEM_LIMIT = 48 * 1024 * 1024

NN = (((1,), (0,)), ((), ()))
NT = (((1,), (1,)), ((), ()))
BNN = (((2,), (1,)), ((0,), (0,)))
BNT = (((2,), (2,)), ((0,), (0,)))


def _dg(a, b, dims):
    return lax.dot_general(a, b, dims, preferred_element_type=F32)


def _split2(x):
    hi = x.astype(BF16)
    lo = (x - hi.astype(F32)).astype(BF16)
    return hi, lo


def _split3(x):
    hi = x.astype(BF16)
    r = x - hi.astype(F32)
    mid = r.astype(BF16)
    lo = (r - mid.astype(F32)).astype(BF16)
    return hi, mid, lo


def _mm(a, b, dims, passes=1):
    if passes == 1:
        return _dg(a.astype(BF16), b.astype(BF16), dims)
    ah, al = _split2(a)
    bh, bl = _split2(b)
    return _dg(ah, bh, dims) + (_dg(ah, bl, dims) + _dg(al, bh, dims))


def _mm_exact_lhs(a_bf16, b, dims):
    hi, mid, lo = _split3(b)
    return _dg(a_bf16, hi, dims) + (_dg(a_bf16, mid, dims) + _dg(a_bf16, lo, dims))


def _mm_exact_rhs(a, b_bf16, dims):
    hi, mid, lo = _split3(a)
    return _dg(hi, b_bf16, dims) + (_dg(mid, b_bf16, dims) + _dg(lo, b_bf16, dims))


def _sigmoid(x):
    return 1.0 / (1.0 + jnp.exp(-x))


def _softplus(x):
    return jnp.maximum(x, 0.0) + jnp.log1p(jnp.exp(-jnp.abs(x)))


def _params(*sem):
    return pltpu.CompilerParams(dimension_semantics=sem, vmem_limit_bytes=VMEM_LIMIT)


def _ada_kernel(c_ref, w_ref, b_ref, o_ref):
    c = c_ref[...]
    cond = c * _sigmoid(c)
    o_ref[0] = _mm(cond, w_ref[0], NN, passes=3) + b_ref[0]


def ada_modulation(c, ada_w, ada_b):
    L, D, D6 = ada_w.shape
    B = c.shape[0]
    rows = 8
    cp = jnp.zeros((rows, D), F32).at[:B].set(c)
    out = pl.pallas_call(
        _ada_kernel,
        grid=(L, D6 // D),
        in_specs=[
            pl.BlockSpec((rows, D), lambda l, j: (0, 0)),
            pl.BlockSpec((1, D, D), lambda l, j: (l, 0, j)),
            pl.BlockSpec((1, 1, D), lambda l, j: (l, 0, j)),
        ],
        out_specs=pl.BlockSpec((1, rows, D), lambda l, j: (l, 0, j)),
        out_shape=jax.ShapeDtypeStruct((L, rows, D6), F32),
        compiler_params=_params("arbitrary", "arbitrary"),
        name="ada_modulation",
    )(cp, ada_w, ada_b.reshape(L, 1, D6))
    return out[:, :B].reshape(L, B, D6 // D, D)


def _rms_modulate(x, gain, scale, shift):
    ms = jnp.mean(x * x, axis=-1, keepdims=True)
    return x * lax.rsqrt(ms + NORM_EPS) * gain * (1.0 + scale) + shift


def _inproj_kernel(x_ref, mod_ref, n1_ref, wm_ref, wf_ref, bf_ref,
                   zr_ref, zp_ref, q_ref, k_ref, vt_ref, cumc_ref, carry_ref,
                   *, tiles_per_batch, tm):
    i = pl.program_id(0)

    @pl.when(i % tiles_per_batch == 0)
    def _():
        carry_ref[...] = jnp.zeros_like(carry_ref)

    mod = mod_ref[0]
    h = _rms_modulate(x_ref[...], n1_ref[...], mod[1:2], mod[0:1])
    z = _dg(h.astype(BF16), wm_ref[...], NN)
    zr_ref[...] = z[:, :RWKV_IN]
    zp_ref[...] = z[:, RWKV_IN:RWKV_IN + POOL_DIM]
    o = RWKV_IN + POOL_DIM
    q_ref[...] = (z[:, o:o + FOX_DIM] * (HEAD_DIM ** -0.5)).astype(BF16)
    k_ref[...] = z[:, o + FOX_DIM:o + 2 * FOX_DIM].astype(BF16)
    vt_ref[0, 0] = z[:, o + 2 * FOX_DIM:o + 3 * FOX_DIM].T.astype(BF16)

    fl = _mm(h, wf_ref[...], NN, passes=3) + bf_ref[...]
    logf = jnp.minimum(fl, 0.0) - jnp.log1p(jnp.exp(-jnp.abs(fl)))
    row = lax.broadcasted_iota(jnp.int32, (tm, tm), 0)
    col = lax.broadcasted_iota(jnp.int32, (tm, tm), 1)
    tri = (row >= col).astype(BF16)
    cum = _mm_exact_lhs(tri, logf, NN) + carry_ref[...]
    carry_ref[...] = cum[tm - 1:tm, :]
    cumc_ref[...] = cum


def in_projection(x2, mod, norm1, w_main, w_f, b_f, *, batch, seq, tm):
    n, d = x2.shape
    tpb = seq // tm
    row = lambda i: (i, 0)
    const = lambda i: (0, 0)
    out_shapes = (
        jax.ShapeDtypeStruct((n, RWKV_IN), F32),
        jax.ShapeDtypeStruct((n, POOL_DIM), F32),
        jax.ShapeDtypeStruct((n, FOX_DIM), BF16),
        jax.ShapeDtypeStruct((n, FOX_DIM), BF16),
        jax.ShapeDtypeStruct((batch, tpb, FOX_DIM, tm), BF16),
        jax.ShapeDtypeStruct((n, LANES), F32),
    )
    return pl.pallas_call(
        functools.partial(_inproj_kernel, tiles_per_batch=tpb, tm=tm),
        grid=(n // tm,),
        in_specs=[
            pl.BlockSpec((tm, d), row),
            pl.BlockSpec((1, 6, d), lambda i: (i // tpb, 0, 0)),
            pl.BlockSpec((1, d), const),
            pl.BlockSpec((d, MAIN_COLS), const),
            pl.BlockSpec((d, LANES), const),
            pl.BlockSpec((1, LANES), const),
        ],
        out_specs=(
            pl.BlockSpec((tm, RWKV_IN), row),
            pl.BlockSpec((tm, POOL_DIM), row),
            pl.BlockSpec((tm, FOX_DIM), row),
            pl.BlockSpec((tm, FOX_DIM), row),
            pl.BlockSpec((1, 1, FOX_DIM, tm), lambda i: (i // tpb, i % tpb, 0, 0)),
            pl.BlockSpec((tm, LANES), row),
        ),
        out_shape=out_shapes,
        scratch_shapes=[pltpu.VMEM((1, LANES), F32)],
        compiler_params=_params("arbitrary"),
        name="in_projection",
    )(x2, mod, norm1, w_main, w_f, b_f)


def _unit_lower_inverse(a, passes):
    c = a.shape[-1]
    ri = lax.broadcasted_iota(jnp.int32, (c, c), 0)
    ci = lax.broadcasted_iota(jnp.int32, (c, c), 1)
    same16 = (ri // 16) == (ci // 16)
    same32 = (ri // 32) == (ci // 32)
    eye = (ri == ci).astype(F32)
    mm = lambda u, v: _mm(u, v, BNN, passes)
    a16 = jnp.where(same16, a, 0.0)
    t = eye + a16
    p = a16
    for _ in range(3):
        p = mm(p, p)
        t = t + mm(t, p)
    o16 = jnp.where(same32 & jnp.logical_not(same16), a, 0.0)
    t = t + mm(t, mm(o16, t))
    o32 = jnp.where(same32, 0.0, a)
    return t + mm(t, mm(o32, t))


def _rwkv_kernel(z_ref, mu_ref, pv_ref, wl_ref, y_ref, prev_ref, st_ref, *, g, passes):
    c = RWKV_CHUNK
    kd = RWKV_DIM
    nh = RWKV_HEADS
    p_score, p_inv, p_apply, p_state = passes

    @pl.when(pl.program_id(0) == 0)
    def _():
        prev_ref[...] = jnp.zeros_like(prev_ref)
        st_ref[...] = jnp.zeros_like(st_ref)

    mu = mu_ref[...]
    row0 = lax.broadcasted_iota(jnp.int32, (c, RWKV_IN), 0) == 0
    zs = []
    for b in range(g):
        zb = z_ref[b]
        prev = jnp.where(row0, prev_ref[b], pltpu.roll(zb, 1, 0))
        prev_ref[b] = zb[c - 1:c, :]
        zs.append(zb + (prev - zb) * mu)
    zf = jnp.stack(zs, axis=0).reshape(g * c, RWKV_IN)

    pv = pv_ref[...]
    w0, a0, k_k, k_a, r_k, ln_w, ln_b = (pv[i:i + 1] for i in range(7))
    r = zf[:, 0:kd]
    k = zf[:, kd:2 * kd]
    v = zf[:, 2 * kd:3 * kd]
    lo = zf[:, 3 * kd:]
    lane = lax.broadcasted_iota(jnp.int32, (1, kd), 1)
    lora_in = jnp.where(lane < 64, jnp.tanh(lo), jnp.where(lane < 128, lo, _sigmoid(lo)))
    lin = _dg(lora_in.astype(BF16), wl_ref[...], NN)
    lw = -jnp.exp(-_softplus(-(w0 + lin[:, 0:kd])) - 0.5)
    a = _sigmoid(a0 + lin[:, kd:2 * kd])
    gate = lin[:, 2 * kd:]

    hrow = lax.broadcasted_iota(jnp.int32, (kd, kd), 0) // HEAD_DIM
    hcol = lax.broadcasted_iota(jnp.int32, (kd, kd), 1) // HEAD_DIM
    same_head = hrow == hcol
    ones_bd = same_head.astype(BF16)
    head_sum = lambda t: _mm_exact_rhs(t, ones_bd, NN)

    kkr = k * k_k
    kk = kkr / jnp.maximum(jnp.sqrt(head_sum(kkr * kkr)), 1e-12)
    kp = k * (1.0 + (a - 1.0) * k_a)
    bonus = head_sum(r * kp * r_k) * v

    to3 = lambda t: t.reshape(g, c, kd)
    r3, kp3, v3, kk3, a3, lw3 = (to3(t) for t in (r, kp, v, kk, a, lw))

    ri = lax.broadcasted_iota(jnp.int32, (c, c), 0)
    ci = lax.broadcasted_iota(jnp.int32, (c, c), 1)
    strict = ri > ci
    incl = ri >= ci
    tri = jnp.broadcast_to(incl.astype(BF16)[None], (g, c, c))
    cum = _mm_exact_lhs(tri, lw3, BNN)
    cum_end = cum[:, c - 1:c, :]
    e_in = jnp.exp(cum)
    e_out = jnp.exp(-cum)
    e_tail = jnp.exp(cum_end - cum)
    p_end = jnp.exp(cum_end)
    beta = kk3 * a3
    ab = -kk3 * jnp.exp(cum - lw3)
    rb = r3 * e_in
    bb = beta * e_out
    kb = kp3 * e_out
    bt = beta * e_tail
    kt = kp3 * e_tail

    hmask = [(lane // HEAD_DIM == h).astype(F32) for h in range(nh)]
    rep = lambda t: jnp.concatenate([t] * nh, axis=0)
    heads = lambda t: jnp.concatenate([t * hmask[h] for h in range(nh)], axis=0)

    def merge(t):
        out = t[0:g] * hmask[0]
        for h in range(1, nh):
            out = out + t[h * g:(h + 1) * g] * hmask[h]
        return out

    lhs = heads(jnp.concatenate([ab, rb], axis=1))
    sb = _mm(lhs, rep(bb), BNT, p_score)
    sk = _mm(lhs, rep(kb), BNT, p_score)
    a_ab = jnp.where(strict, sb[:, :c, :], 0.0)
    a_rb = jnp.where(incl, sb[:, c:, :], 0.0)
    a_ak = jnp.where(strict, sk[:, :c, :], 0.0)
    a_rk = jnp.where(incl, sk[:, c:, :], 0.0)
    t_inv = _unit_lower_inverse(a_ab, p_inv)

    app = lambda u, w: _mm(u, w, BNN, p_apply)
    v_rep = rep(v3)
    akv = app(a_ak, v_rep)
    w_h = app(t_inv, rep(ab))
    u0_h = app(t_inv, akv)
    w_all = merge(w_h)
    u0_all = merge(u0_h)
    rq = rb + merge(app(a_rb, w_h))
    yc = merge(app(a_rb, u0_h) + app(a_rk, v_rep))

    bt_t = jnp.swapaxes(bt, 1, 2)
    kt_t = jnp.swapaxes(kt, 1, 2)
    eye = (lax.broadcasted_iota(jnp.int32, (kd, kd), 0)
           == lax.broadcasted_iota(jnp.int32, (kd, kd), 1))
    trans = jnp.where(same_head, app(bt_t, w_all), 0.0) + jnp.where(eye, p_end, 0.0)
    inject = jnp.where(same_head, app(bt_t, u0_all) + app(kt_t, v3), 0.0)

    st = st_ref[...]
    y = _mm(rq, st, BNN, p_state) + yc
    st_ref[...] = _mm(trans, st, BNN, p_state) + inject

    y2 = y.reshape(g * c, kd)
    mean = head_sum(y2) * (1.0 / HEAD_DIM)
    dv = y2 - mean
    var = head_sum(dv * dv) * (1.0 / HEAD_DIM)
    out = (dv * lax.rsqrt(var + RWKV_GN_EPS) * ln_w + ln_b + bonus) * gate
    y_ref[...] = out.reshape(g, c, kd).astype(y_ref.dtype)


def rwkv_mixer(z3, mu, pvec, w_lora, *, passes=(3, 3, 3, 3)):
    b, s, _ = z3.shape
    c = RWKV_CHUNK
    return pl.pallas_call(
        functools.partial(_rwkv_kernel, g=b, passes=passes),
        grid=(s // c,),
        in_specs=[
            pl.BlockSpec((b, c, RWKV_IN), lambda t: (0, t, 0)),
            pl.BlockSpec((1, RWKV_IN), lambda t: (0, 0)),
            pl.BlockSpec((8, RWKV_DIM), lambda t: (0, 0)),
            pl.BlockSpec((RWKV_LORA_IN, 3 * RWKV_DIM), lambda t: (0, 0)),
        ],
        out_specs=pl.BlockSpec((b, c, RWKV_DIM), lambda t: (0, t, 0)),
        out_shape=jax.ShapeDtypeStruct((b, s, RWKV_DIM), BF16),
        scratch_shapes=[
            pltpu.VMEM((b, 1, RWKV_IN), F32),
            pltpu.VMEM((b, RWKV_DIM, RWKV_DIM), F32),
        ],
        compiler_params=_params("arbitrary"),
        name="rwkv_mixer",
    )(z3, mu, pvec, w_lora)


def _pool_kernel(z_ref, w_ref, sc_ref, y_ref, ext_ref, *, tp):
    t = pl.program_id(1)
    hl = POOL_HALO
    levels = len(POOL_WINDOWS)

    @pl.when(t == 0)
    def _():
        ext_ref[:, 0:hl, :] = jnp.zeros((levels, hl, POOL_DIM), F32)

    z = z_ref[0]
    sums = []
    cur = z
    for lvl, w in enumerate(POOL_WINDOWS):
        ext_ref[lvl, hl:hl + tp, :] = cur
        back = w // 2
        cur = cur + ext_ref[lvl, hl - back:hl - back + tp, :]
        sums.append(cur)
    for lvl in range(levels):
        ext_ref[lvl, 0:hl, :] = ext_ref[lvl, tp:tp + hl, :]

    lane = lax.broadcasted_iota(jnp.int32, (tp, POOL_DIM), 1)
    pos = (t * tp + lax.broadcasted_iota(jnp.int32, (tp, POOL_DIM), 0) + 1).astype(F32)
    grp = POOL_DIM // levels
    win_sum = sums[-1]
    count = jnp.minimum(pos, float(POOL_WINDOWS[-1]))
    for lvl in range(levels - 2, -1, -1):
        in_grp = lane < (lvl + 1) * grp
        win_sum = jnp.where(in_grp, sums[lvl], win_sum)
        count = jnp.where(in_grp, jnp.minimum(pos, float(POOL_WINDOWS[lvl])), count)
    pooled = win_sum / count - z
    y = _dg(pooled.astype(BF16), w_ref[...], NN) * sc_ref[...]
    y_ref[0] = y.astype(y_ref.dtype)


def pool_mixer(zp3, w_bd, scale, *, tp):
    b, s, _ = zp3.shape
    return pl.pallas_call(
        functools.partial(_pool_kernel, tp=tp),
        grid=(b, s // tp),
        in_specs=[
            pl.BlockSpec((1, tp, POOL_DIM), lambda i, t: (i, t, 0)),
            pl.BlockSpec((POOL_DIM, POOL_DIM), lambda i, t: (0, 0)),
            pl.BlockSpec((1, POOL_DIM), lambda i, t: (0, 0)),
        ],
        out_specs=pl.BlockSpec((1, tp, POOL_DIM), lambda i, t: (i, t, 0)),
        out_shape=jax.ShapeDtypeStruct((b, s, POOL_DIM), BF16),
        scratch_shapes=[pltpu.VMEM((len(POOL_WINDOWS), POOL_HALO + tp, POOL_DIM), F32)],
        compiler_params=_params("arbitrary", "arbitrary"),
        name="pool_mixer",
    )(zp3, w_bd, scale)


def _fox_kernel(q_ref, k_ref, vt_ref, cc_ref, o_ref, ckb_ref, m_ref, l_ref, acc_ref, *, tq, nblk):
    hp = pl.program_id(1)
    i = pl.program_id(2)
    lane = lax.broadcasted_iota(jnp.int32, (1, LANES), 1)
    reps = tq // LANES

    @pl.when(i == 0)
    def _():
        def fill(jb, carry):
            start = pl.multiple_of(jb * tq, tq)
            blk = cc_ref[0, pl.ds(start, tq), :]
            for hh in range(2):
                col = jnp.sum(jnp.where(lane == 2 * hp + hh, blk, 0.0), axis=-1, keepdims=True)
                ckb_ref[hh, pl.ds(start, tq), :] = jnp.broadcast_to(col, (tq, LANES))
            return carry

        lax.fori_loop(0, nblk, fill, 0)

    q2 = q_ref[0]
    m_ref[...] = jnp.full(m_ref.shape, NEG_BIG, F32)
    l_ref[...] = jnp.zeros(l_ref.shape, F32)
    acc_ref[...] = jnp.zeros(acc_ref.shape, F32)
    qh = [jnp.where(lane // HEAD_DIM == hh, q2, jnp.zeros_like(q2)) for hh in range(2)]

    def step(j, masked):
        start = pl.multiple_of(j * tq, tq)
        k2 = k_ref[0, pl.ds(start, tq), :]
        for hh in range(2):
            s = _dg(k2, qh[hh], NT)
            ck = ckb_ref[hh, pl.ds(start, tq), :]
            s = s - jnp.concatenate([ck] * reps, axis=1)
            if masked:
                kpos = lax.broadcasted_iota(jnp.int32, (tq, tq), 0)
                qpos = lax.broadcasted_iota(jnp.int32, (tq, tq), 1)
                s = jnp.where(kpos <= qpos, s, NEG_BIG)
            m_old = m_ref[hh][0:1, :]
            m_new = jnp.maximum(m_old, jnp.max(s, axis=0, keepdims=True))
            alpha = jnp.exp(m_old - m_new)
            p = jnp.exp(s - m_new)
            l_new = alpha * l_ref[hh][0:1, :] + jnp.sum(p, axis=0, keepdims=True)
            vth = vt_ref[0, j, hh * HEAD_DIM:(hh + 1) * HEAD_DIM, :]
            acc_ref[hh] = alpha * acc_ref[hh] + _dg(vth, p.astype(BF16), NN)
            l_ref[hh] = jnp.broadcast_to(l_new, (8, tq))
            m_ref[hh] = jnp.broadcast_to(m_new, (8, tq))

    def body(j, carry):
        step(j, False)
        return carry

    lax.fori_loop(0, i, body, 0)
    step(i, True)
    ot = jnp.concatenate([acc_ref[hh] / l_ref[hh][0:1, :] for hh in range(2)], axis=0)
    o_ref[0] = ot.T.astype(o_ref.dtype)


def fox_attention(q3, k3, vt4, cumc3, *, tq):
    b, s, _ = q3.shape
    nq = s // tq
    pairs = FOX_HEADS // 2
    return pl.pallas_call(
        functools.partial(_fox_kernel, tq=tq, nblk=nq),
        grid=(b, pairs, nq),
        in_specs=[
            pl.BlockSpec((1, tq, LANES), lambda bi, hp, i: (bi, i, hp)),
            pl.BlockSpec((1, s, LANES), lambda bi, hp, i: (bi, 0, hp)),
            pl.BlockSpec((1, nq, LANES, tq), lambda bi, hp, i: (bi, 0, hp, 0)),
            pl.BlockSpec((1, s, LANES), lambda bi, hp, i: (bi, 0, 0)),
        ],
        out_specs=pl.BlockSpec((1, tq, LANES), lambda bi, hp, i: (bi, i, hp)),
        out_shape=jax.ShapeDtypeStruct((b, s, FOX_DIM), BF16),
        scratch_shapes=[
            pltpu.VMEM((2, s, LANES), F32),
            pltpu.VMEM((2, 8, tq), F32),
            pltpu.VMEM((2, 8, tq), F32),
            pltpu.VMEM((2, HEAD_DIM, tq), F32),
        ],
        compiler_params=_params("arbitrary", "arbitrary", "arbitrary"),
        name="fox_attention",
    )(q3, k3, vt4, cumc3)


def _outproj_kernel(yr_ref, yp_ref, yf_ref, x_ref, mod_ref, wo_ref, n2_ref, rw_ref, rb_ref,
                    x1_ref, h2_ref, ti_ref, tg_ref):
    mod = mod_ref[0]
    mixed = (_dg(yr_ref[...], wo_ref[0:RWKV_DIM, :], NN)
             + _dg(yp_ref[...], wo_ref[RWKV_DIM:RWKV_DIM + POOL_DIM, :], NN)
             + _dg(yf_ref[...], wo_ref[RWKV_DIM + POOL_DIM:, :], NN))
    x1 = x_ref[...] + mod[2:3] * mixed
    x1_ref[...] = x1
    h2 = _rms_modulate(x1, n2_ref[...], mod[4:5], mod[3:4])
    h2_ref[...] = h2

    tm = x1.shape[0]
    lane = lax.broadcasted_iota(jnp.int32, (tm, LANES), 1)
    lane_f = lane.astype(F32)
    logits = _mm(h2, rw_ref[...], NN, passes=3) + rb_ref[...]
    work = jnp.where(lane < N_EXPERTS, logits, NEG_BIG)
    top_i = jnp.zeros((tm, LANES), F32)
    top_e = jnp.zeros((tm, LANES), F32)
    v0 = None
    for kk in range(TOP_K):
        vmax = jnp.max(work, axis=-1, keepdims=True)
        idx = jnp.min(jnp.where(work == vmax, lane_f, float(LANES)), axis=-1, keepdims=True)
        if kk == 0:
            v0 = vmax
        top_i = jnp.where(lane == kk, idx, top_i)
        top_e = jnp.where(lane == kk, jnp.exp(vmax - v0), top_e)
        work = jnp.where(lane_f == idx, NEG_BIG, work)
    ti_ref[...] = top_i.astype(jnp.int32)
    tg_ref[...] = top_e / jnp.sum(top_e, axis=-1, keepdims=True)


def out_projection(yr, yp, yf, x2, mod, w_out, norm2, rw, rb, *, seq, tm):
    n, d = x2.shape
    tpb = seq // tm
    row = lambda i: (i, 0)
    const = lambda i: (0, 0)
    return pl.pallas_call(
        _outproj_kernel,
        grid=(n // tm,),
        in_specs=[
            pl.BlockSpec((tm, RWKV_DIM), row),
            pl.BlockSpec((tm, POOL_DIM), row),
            pl.BlockSpec((tm, FOX_DIM), row),
            pl.BlockSpec((tm, d), row),
            pl.BlockSpec((1, 6, d), lambda i: (i // tpb, 0, 0)),
            pl.BlockSpec((d, d), const),
            pl.BlockSpec((1, d), const),
            pl.BlockSpec((d, LANES), const),
            pl.BlockSpec((1, LANES), const),
        ],
        out_specs=(
            pl.BlockSpec((tm, d), row),
            pl.BlockSpec((tm, d), row),
            pl.BlockSpec((tm, LANES), row),
            pl.BlockSpec((tm, LANES), row),
        ),
        out_shape=(
            jax.ShapeDtypeStruct((n, d), F32),
            jax.ShapeDtypeStruct((n, d), F32),
            jax.ShapeDtypeStruct((n, LANES), jnp.int32),
            jax.ShapeDtypeStruct((n, LANES), F32),
        ),
        compiler_params=_params("arbitrary"),
        name="out_projection",
    )(yr, yp, yf, x2, mod, w_out, norm2, rw, rb)


def _routing_tables(top_i, *, rows_per_block):
    n = top_i.shape[0]
    r = rows_per_block
    e_flat = top_i.reshape(n * TOP_K)
    onehot = (e_flat[:, None] == jnp.arange(N_EXPERTS, dtype=jnp.int32)[None, :]).astype(jnp.int32)
    csum = jnp.cumsum(onehot, axis=0)
    rank = jnp.sum(csum * onehot, axis=1) - 1
    counts = csum[-1]
    padded = (counts + r - 1) // r * r
    pend = jnp.cumsum(padded)
    pstart = pend - padded
    dest = (pstart[e_flat] + rank).astype(jnp.int32)
    nb = -(-(n * TOP_K + N_EXPERTS * (r - 1)) // r)
    bstart = jnp.arange(nb, dtype=jnp.int32) * r
    block_e = jnp.minimum(jnp.searchsorted(pend, bstart, side='right'), N_EXPERTS - 1).astype(jnp.int32)
    nvalid = jnp.clip(pstart[block_e] + counts[block_e] - bstart, 0, r).astype(jnp.int32)
    nused = (pend[-1] // r).astype(jnp.int32).reshape(1)
    return dest, block_e, nvalid, nused, nb


def _dispatch_kernel(dest_ref, h_ref, xs_in_ref, xs_ref, sem, *, td):
    del xs_in_ref
    def issue(t, carry):
        for kk in range(TOP_K):
            pltpu.make_async_copy(h_ref.at[pl.ds(t, 1)],
                                  xs_ref.at[pl.ds(dest_ref[0, 0, TOP_K * t + kk], 1)], sem).start()
        return carry

    lax.fori_loop(0, td, issue, 0, unroll=8)
    for kk in range(TOP_K):
        pltpu.make_async_copy(h_ref, xs_ref.at[pl.ds(0, td)], sem).wait()


def moe_dispatch(h2, dest2, *, rows, td):
    n, d = h2.shape
    return pl.pallas_call(
        functools.partial(_dispatch_kernel, td=td),
        grid=(n // td,),
        in_specs=[
            pl.BlockSpec((1, 1, TOP_K * td), lambda i: (i, 0, 0), memory_space=pltpu.SMEM),
            pl.BlockSpec((td, d), lambda i: (i, 0)),
            pl.BlockSpec(memory_space=pl.ANY),
        ],
        out_specs=pl.BlockSpec(memory_space=pl.ANY),
        out_shape=jax.ShapeDtypeStruct((rows, d), F32),
        scratch_shapes=[pltpu.SemaphoreType.DMA],
        input_output_aliases={2: 0},
        compiler_params=_params("arbitrary"),
        name="moe_dispatch",
    )(dest2, h2, jnp.zeros((rows, d), F32))


def _expert_kernel(be_ref, nv_ref, nu_ref, x_ref, wgu_ref, bgu_ref, wd_ref, bd_ref, y_ref,
                   wgu_bf, wd_bf, cur_ref):
    b = pl.program_id(0)
    e = be_ref[b]
    nv = nv_ref[b]

    @pl.when(b == 0)
    def _():
        cur_ref[0] = -1

    @pl.when((nv > 0) & (cur_ref[0] != e))
    def _():
        cur_ref[0] = e
        step = 128

        def cast(i, carry):
            s = pl.multiple_of(i * step, step)
            wgu_bf[pl.ds(s, step), :] = wgu_ref[0, 0, pl.ds(s, step), :].astype(BF16)
            wd_bf[pl.ds(s, step), :] = wd_ref[0, 0, pl.ds(s, step), :].astype(BF16)
            return carry

        lax.fori_loop(0, D_MODEL // step, cast, 0)

    @pl.when(nv == 0)
    def _():
        y_ref[...] = jnp.zeros_like(y_ref)

    @pl.when(nv > 0)
    def _():
        x = x_ref[...].astype(BF16)
        gu = _dg(x, wgu_bf[...], NN) + bgu_ref[0, 0]
        gate = jnp.minimum(gu[:, :D_FF], SWIGLU_LIMIT)
        up = jnp.clip(gu[:, D_FF:], -SWIGLU_LIMIT, SWIGLU_LIMIT)
        act = (up + 1.0) * (gate * _sigmoid(SWIGLU_ALPHA * gate))
        y_ref[...] = _dg(act.astype(BF16), wd_bf[...], NN) + bd_ref[0, 0]


def moe_experts(xs, block_e, nvalid, nused, w_gu, b_gu, w_down, b_down, *, layer, r):
    rows, d = xs.shape
    nb = rows // r
    blk = lambda b, be, nv, nu: (jnp.minimum(b, nu[0] - 1), 0)
    wsel = lambda b, be, nv, nu: (layer, be[b], 0, 0)
    grid_spec = pltpu.PrefetchScalarGridSpec(
        num_scalar_prefetch=3,
        grid=(nb,),
        in_specs=[
            pl.BlockSpec((r, d), blk),
            pl.BlockSpec((1, 1, d, 2 * D_FF), wsel),
            pl.BlockSpec((1, 1, 1, 2 * D_FF), wsel),
            pl.BlockSpec((1, 1, D_FF, d), wsel),
            pl.BlockSpec((1, 1, 1, d), wsel),
        ],
        out_specs=pl.BlockSpec((r, d), lambda b, be, nv, nu: (b, 0)),
        scratch_shapes=[
            pltpu.VMEM((d, 2 * D_FF), BF16),
            pltpu.VMEM((D_FF, d), BF16),
            pltpu.SMEM((1,), jnp.int32),
        ],
    )
    return pl.pallas_call(
        _expert_kernel,
        grid_spec=grid_spec,
        out_shape=jax.ShapeDtypeStruct((rows, d), F32),
        compiler_params=_params("arbitrary"),
        name="moe_experts",
    )(block_e, nvalid, nused, xs, w_gu, b_gu, w_down, b_down)


def _combine_kernel(dest_ref, yb_ref, tg_ref, x_ref, mod_ref, nf_ref, o_ref, buf, sem, *, tc, final_norm):
    def issue(t, carry):
        for kk in range(TOP_K):
            pltpu.make_async_copy(yb_ref.at[pl.ds(dest_ref[0, 0, TOP_K * t + kk], 1)],
                                  buf.at[kk, pl.ds(t, 1)], sem).start()
        return carry

    lax.fori_loop(0, tc, issue, 0, unroll=8)
    for kk in range(TOP_K):
        pltpu.make_async_copy(yb_ref.at[pl.ds(0, tc)], buf.at[kk], sem).wait()

    tg = tg_ref[...]
    y = tg[:, 0:1] * buf[0]
    for kk in range(1, TOP_K):
        y = y + tg[:, kk:kk + 1] * buf[kk]
    x = x_ref[...] + mod_ref[0][5:6] * y
    if final_norm:
        ms = jnp.mean(x * x, axis=-1, keepdims=True)
        x = x * lax.rsqrt(ms + NORM_EPS) * nf_ref[...]
    o_ref[...] = x


def moe_combine(yb, dest2, tg, x1, mod, norm_f, *, seq, tc, final_norm):
    n, d = x1.shape
    tpb = seq // tc
    row = lambda i: (i, 0)
    return pl.pallas_call(
        functools.partial(_combine_kernel, tc=tc, final_norm=final_norm),
        grid=(n // tc,),
        in_specs=[
            pl.BlockSpec((1, 1, TOP_K * tc), lambda i: (i, 0, 0), memory_space=pltpu.SMEM),
            pl.BlockSpec(memory_space=pl.ANY),
            pl.BlockSpec((tc, LANES), row),
            pl.BlockSpec((tc, d), row),
            pl.BlockSpec((1, 6, d), lambda i: (i // tpb, 0, 0)),
            pl.BlockSpec((1, d), lambda i: (0, 0)),
        ],
        out_specs=pl.BlockSpec((tc, d), row),
        out_shape=jax.ShapeDtypeStruct((n, d), F32),
        scratch_shapes=[pltpu.VMEM((TOP_K, tc, d), F32), pltpu.SemaphoreType.DMA],
        compiler_params=_params("arbitrary"),
        name="moe_combine",
    )(dest2, yb, tg, x1, mod, norm_f)


def _block_diag(blocks):
    g, m, _ = blocks.shape
    out = jnp.zeros((g * m, g * m), blocks.dtype)
    for i in range(g):
        out = out.at[i * m:(i + 1) * m, i * m:(i + 1) * m].set(blocks[i])
    return out


def _pad_cols(w, cols):
    return jnp.zeros(w.shape[:-1] + (cols,), w.dtype).at[..., :w.shape[-1]].set(w)


def trunk(x, c, ada_w, ada_b, norm1, w_in, w_out, rwkv_mu, rwkv_w0, rwkv_w_up, rwkv_a0, rwkv_a_up,
          rwkv_g_up, rwkv_k_k, rwkv_k_a, rwkv_r_k, rwkv_ln_w, rwkv_ln_b, pool_w, pool_scale, fox_b_f,
          norm2, router_w, router_b, moe_w_gu, moe_b_gu, moe_w_down, moe_b_down, norm_f,
          *, tm=512, tp=512, tq=512, expert_rows=256, td=512, tc=256, rwkv_passes=(1, 3, 1, 3)):
    batch, seq, d = x.shape
    assert tm == tq, "the attention kernel reads the forget sums in the in-projection's time tiling"
    depth = ada_w.shape[0]
    n = batch * seq
    mod_all = ada_modulation(c, ada_w, ada_b)
    x2 = x.reshape(n, d)
    nf = norm_f.reshape(1, d)
    b_gu4 = moe_b_gu.reshape(depth, N_EXPERTS, 1, 2 * D_FF)
    b_down4 = moe_b_down.reshape(depth, N_EXPERTS, 1, d)
    for l in range(depth):
        mod = mod_all[l]
        w_main = w_in[l][:, :MAIN_COLS].astype(BF16)
        w_f = _pad_cols(w_in[l][:, MAIN_COLS:], LANES)
        b_f = _pad_cols(fox_b_f[l].reshape(1, FOX_HEADS), LANES)
        w_lora = jnp.zeros((RWKV_LORA_IN, 3 * RWKV_DIM), F32)
        w_lora = w_lora.at[0:64, 0:RWKV_DIM].set(rwkv_w_up[l])
        w_lora = w_lora.at[64:128, RWKV_DIM:2 * RWKV_DIM].set(rwkv_a_up[l])
        w_lora = w_lora.at[128:256, 2 * RWKV_DIM:].set(rwkv_g_up[l]).astype(BF16)
        pvec = jnp.stack([rwkv_w0[l], rwkv_a0[l], rwkv_k_k[l], rwkv_k_a[l], rwkv_r_k[l].reshape(-1),
                          rwkv_ln_w[l], rwkv_ln_b[l], jnp.zeros((RWKV_DIM,), F32)], axis=0)
        w_pool = _block_diag(pool_w[l]).astype(BF16)
        rw = _pad_cols(router_w[l], LANES)
        rb = _pad_cols(router_b[l].reshape(1, N_EXPERTS), LANES)

        zr, zp, q, k, vt, cumc = in_projection(
            x2, mod, norm1[l].reshape(1, d), w_main, w_f, b_f, batch=batch, seq=seq, tm=tm)
        yr = rwkv_mixer(zr.reshape(batch, seq, RWKV_IN), rwkv_mu[l].reshape(1, RWKV_IN), pvec, w_lora,
                        passes=rwkv_passes)
        yp = pool_mixer(zp.reshape(batch, seq, POOL_DIM), w_pool, pool_scale[l].reshape(1, POOL_DIM), tp=tp)
        to3 = lambda t: t.reshape(batch, seq, t.shape[-1])
        yf = fox_attention(to3(q), to3(k), vt, to3(cumc), tq=tq)
        x1, h2, top_i, top_g = out_projection(
            yr.reshape(n, RWKV_DIM), yp.reshape(n, POOL_DIM), yf.reshape(n, FOX_DIM), x2, mod,
            w_out[l].astype(BF16), norm2[l].reshape(1, d), rw, rb, seq=seq, tm=tm)

        dest, block_e, nvalid, nused, nb = _routing_tables(top_i[:, :TOP_K], rows_per_block=expert_rows)
        xs = moe_dispatch(h2, dest.reshape(n // td, 1, TOP_K * td), rows=nb * expert_rows, td=td)
        yb = moe_experts(xs, block_e, nvalid, nused, moe_w_gu, b_gu4, moe_w_down, b_down4,
                         layer=l, r=expert_rows)
        x2 = moe_combine(yb, dest.reshape(n // tc, 1, TOP_K * tc), top_g, x1, mod, nf, seq=seq, tc=tc,
                         final_norm=(l == depth - 1))
    return x2.reshape(batch, seq, d)


def kernel(x, c, ada_w, ada_b, norm1, w_in, w_out, rwkv_mu, rwkv_w0, rwkv_w_up, rwkv_a0, rwkv_a_up, rwkv_g_up, rwkv_k_k, rwkv_k_a, rwkv_r_k, rwkv_ln_w, rwkv_ln_b, pool_w, pool_scale, fox_b_f, norm2, router_w, router_b, moe_w_gu, moe_b_gu, moe_w_down, moe_b_down, norm_f):
    return trunk(x, c, ada_w, ada_b, norm1, w_in, w_out, rwkv_mu, rwkv_w0, rwkv_w_up, rwkv_a0, rwkv_a_up,
                 rwkv_g_up, rwkv_k_k, rwkv_k_a, rwkv_r_k, rwkv_ln_w, rwkv_ln_b, pool_w, pool_scale, fox_b_f,
                 norm2, router_w, router_b, moe_w_gu, moe_b_gu, moe_w_down, moe_b_down, norm_f)
```

```python
import functools

import jax
import jax.numpy as jnp
from jax import lax
from jax.experimental import pallas as pl
from jax.experimental.pallas import tpu as pltpu

F32 = jnp.float32
BF16 = jnp.bfloat16

D_MODEL = 1024
HEAD_DIM = 64
RWKV_DIM = 256
RWKV_HEADS = 4
RWKV_LORA_IN = 256
RWKV_GN_EPS = 64e-5
POOL_DIM = 256
POOL_WINDOWS = (2, 4, 8, 16)
POOL_HALO = 16
FOX_DIM = 512
FOX_HEADS = 8
N_EXPERTS = 32
TOP_K = 4
D_FF = 1024
SWIGLU_LIMIT = 7.0
SWIGLU_ALPHA = 1.702
NORM_EPS = 1e-6
RWKV_IN = 1024
MAIN_COLS = RWKV_IN + POOL_DIM + 3 * FOX_DIM
LANES = 128
RWKV_CHUNK = 64
NEG_BIG = -1e30
VMEM_LIMIT = 48 * 1024 * 1024

NN = (((1,), (0,)), ((), ()))
NT = (((1,), (1,)), ((), ()))
BNN = (((2,), (1,)), ((0,), (0,)))
BNT = (((2,), (2,)), ((0,), (0,)))


def _dg(a, b, dims):
    return lax.dot_general(a, b, dims, preferred_element_type=F32)


def _split2(x):
    hi = x.astype(BF16)
    lo = (x - hi.astype(F32)).astype(BF16)
    return hi, lo


def _split3(x):
    hi = x.astype(BF16)
    r = x - hi.astype(F32)
    mid = r.astype(BF16)
    lo = (r - mid.astype(F32)).astype(BF16)
    return hi, mid, lo


def _mm(a, b, dims, passes=1):
    if passes == 1:
        return _dg(a.astype(BF16), b.astype(BF16), dims)
    ah, al = _split2(a)
    bh, bl = _split2(b)
    return _dg(ah, bh, dims) + (_dg(ah, bl, dims) + _dg(al, bh, dims))


def _mm_exact_lhs(a_bf16, b, dims):
    hi, mid, lo = _split3(b)
    return _dg(a_bf16, hi, dims) + (_dg(a_bf16, mid, dims) + _dg(a_bf16, lo, dims))


def _mm_exact_rhs(a, b_bf16, dims):
    hi, mid, lo = _split3(a)
    return _dg(hi, b_bf16, dims) + (_dg(mid, b_bf16, dims) + _dg(lo, b_bf16, dims))


def _sigmoid(x):
    return 1.0 / (1.0 + jnp.exp(-x))


def _softplus(x):
    return jnp.maximum(x, 0.0) + jnp.log1p(jnp.exp(-jnp.abs(x)))


def _params(*sem):
    return pltpu.CompilerParams(dimension_semantics=sem, vmem_limit_bytes=VMEM_LIMIT)


def _ada_kernel(c_ref, w_ref, b_ref, o_ref):
    c = c_ref[...]
    cond = c * _sigmoid(c)
    o_ref[0] = _mm(cond, w_ref[0], NN, passes=3) + b_ref[0]


def ada_modulation(c, ada_w, ada_b):
    L, D, D6 = ada_w.shape
    B = c.shape[0]
    rows = 8
    cp = jnp.zeros((rows, D), F32).at[:B].set(c)
    out = pl.pallas_call(
        _ada_kernel,
        grid=(L, D6 // D),
        in_specs=[
            pl.BlockSpec((rows, D), lambda l, j: (0, 0)),
            pl.BlockSpec((1, D, D), lambda l, j: (l, 0, j)),
            pl.BlockSpec((1, 1, D), lambda l, j: (l, 0, j)),
        ],
        out_specs=pl.BlockSpec((1, rows, D), lambda l, j: (l, 0, j)),
        out_shape=jax.ShapeDtypeStruct((L, rows, D6), F32),
        compiler_params=_params("arbitrary", "arbitrary"),
        name="ada_modulation",
    )(cp, ada_w, ada_b.reshape(L, 1, D6))
    return out[:, :B].reshape(L, B, D6 // D, D)


def _rms_modulate(x, gain, scale, shift):
    ms = jnp.mean(x * x, axis=-1, keepdims=True)
    return x * lax.rsqrt(ms + NORM_EPS) * gain * (1.0 + scale) + shift


def _inproj_kernel(x_ref, mod_ref, n1_ref, wm_ref, wf_ref, bf_ref,
                   zr_ref, zp_ref, q_ref, k_ref, vt_ref, cumc_ref, carry_ref,
                   *, tiles_per_batch, tm):
    i = pl.program_id(0)

    @pl.when(i % tiles_per_batch == 0)
    def _():
        carry_ref[...] = jnp.zeros_like(carry_ref)

    mod = mod_ref[0]
    h = _rms_modulate(x_ref[...], n1_ref[...], mod[1:2], mod[0:1])
    z = _dg(h.astype(BF16), wm_ref[...], NN)
    zr_ref[...] = z[:, :RWKV_IN]
    zp_ref[...] = z[:, RWKV_IN:RWKV_IN + POOL_DIM]
    o = RWKV_IN + POOL_DIM
    q_ref[...] = (z[:, o:o + FOX_DIM] * (HEAD_DIM ** -0.5)).astype(BF16)
    k_ref[...] = z[:, o + FOX_DIM:o + 2 * FOX_DIM].astype(BF16)
    vt_ref[0, 0] = z[:, o + 2 * FOX_DIM:o + 3 * FOX_DIM].T.astype(BF16)

    fl = _mm(h, wf_ref[...], NN, passes=3) + bf_ref[...]
    logf = jnp.minimum(fl, 0.0) - jnp.log1p(jnp.exp(-jnp.abs(fl)))
    row = lax.broadcasted_iota(jnp.int32, (tm, tm), 0)
    col = lax.broadcasted_iota(jnp.int32, (tm, tm), 1)
    tri = (row >= col).astype(BF16)
    cum = _mm_exact_lhs(tri, logf, NN) + carry_ref[...]
    carry_ref[...] = cum[tm - 1:tm, :]
    cumc_ref[...] = cum


def in_projection(x2, mod, norm1, w_main, w_f, b_f, *, batch, seq, tm):
    n, d = x2.shape
    tpb = seq // tm
    row = lambda i: (i, 0)
    const = lambda i: (0, 0)
    out_shapes = (
        jax.ShapeDtypeStruct((n, RWKV_IN), F32),
        jax.ShapeDtypeStruct((n, POOL_DIM), F32),
        jax.ShapeDtypeStruct((n, FOX_DIM), BF16),
        jax.ShapeDtypeStruct((n, FOX_DIM), BF16),
        jax.ShapeDtypeStruct((batch, tpb, FOX_DIM, tm), BF16),
        jax.ShapeDtypeStruct((n, LANES), F32),
    )
    return pl.pallas_call(
        functools.partial(_inproj_kernel, tiles_per_batch=tpb, tm=tm),
        grid=(n // tm,),
        in_specs=[
            pl.BlockSpec((tm, d), row),
            pl.BlockSpec((1, 6, d), lambda i: (i // tpb, 0, 0)),
            pl.BlockSpec((1, d), const),
            pl.BlockSpec((d, MAIN_COLS), const),
            pl.BlockSpec((d, LANES), const),
            pl.BlockSpec((1, LANES), const),
        ],
        out_specs=(
            pl.BlockSpec((tm, RWKV_IN), row),
            pl.BlockSpec((tm, POOL_DIM), row),
            pl.BlockSpec((tm, FOX_DIM), row),
            pl.BlockSpec((tm, FOX_DIM), row),
            pl.BlockSpec((1, 1, FOX_DIM, tm), lambda i: (i // tpb, i % tpb, 0, 0)),
            pl.BlockSpec((tm, LANES), row),
        ),
        out_shape=out_shapes,
        scratch_shapes=[pltpu.VMEM((1, LANES), F32)],
        compiler_params=_params("arbitrary"),
        name="in_projection",
    )(x2, mod, norm1, w_main, w_f, b_f)


def _head_blocks(y, same_head):
    return jnp.concatenate([y] * RWKV_HEADS, axis=1) * same_head


def _mm_shared(xs, y, dims, passes):
    sizes = [x.shape[1] for x in xs]
    x = xs[0] if len(xs) == 1 else jnp.concatenate(xs, axis=1)
    m = x.shape[1]
    if passes == 1:
        out = _dg(x.astype(BF16), y.astype(BF16), dims)
    else:
        xh, xl = _split2(x)
        yh, yl = y if isinstance(y, tuple) else _split2(y)
        both = _dg(jnp.concatenate([xh, xl], axis=1), yh, dims)
        out = both[:, :m] + (both[:, m:] + _dg(xh, yl, dims))
    outs, start = [], 0
    for s in sizes:
        outs.append(out[:, start:start + s])
        start += s
    return outs


def _mm_heads(xs, y, same_head, passes, dims):
    if passes == 1:
        return _mm_shared(xs, _head_blocks(y.astype(BF16), same_head), dims, 1)
    yh, yl = _split2(y)
    return _mm_shared(xs, (_head_blocks(yh, same_head), _head_blocks(yl, same_head)), dims, passes)


def _unit_lower_inverse(a, same_head, passes):
    c = a.shape[-2]
    ri = lax.broadcasted_iota(jnp.int32, a.shape[-2:], 0)
    ci = lax.broadcasted_iota(jnp.int32, a.shape[-2:], 1) % c
    same16 = (ri // 16) == (ci // 16)
    same32 = (ri // 32) == (ci // 32)
    eye = (ri == ci).astype(F32)
    mm = lambda us, v: _mm_heads(us, v, same_head, passes, BNN)
    a16 = jnp.where(same16, a, 0.0)
    t = eye + a16
    (p,) = mm([a16], a16)
    for _ in range(2):
        tp, p2 = mm([t, p], p)
        t = t + tp
        p = p2
    t = t + mm([t], p)[0]
    o16 = jnp.where(same32 & jnp.logical_not(same16), a, 0.0)
    t = t + mm([t], mm([o16], t)[0])[0]
    o32 = jnp.where(same32, 0.0, a)
    return t + mm([t], mm([o32], t)[0])[0]


def _rwkv_kernel(z_ref, mu_ref, pv_ref, wl_ref, y_ref, prev_ref, st_ref, *, nb, nc, passes):
    c = RWKV_CHUNK
    kd = RWKV_DIM
    g = nb * nc
    rows = nc * c
    p_score, p_inv, p_apply, p_state = passes

    @pl.when(pl.program_id(0) == 0)
    def _():
        prev_ref[...] = jnp.zeros_like(prev_ref)
        st_ref[...] = jnp.zeros_like(st_ref)

    mu = mu_ref[...]
    row0 = lax.broadcasted_iota(jnp.int32, (rows, RWKV_IN), 0) == 0
    zs = []
    for b in range(nb):
        zb = z_ref[b]
        prev = jnp.where(row0, prev_ref[b], pltpu.roll(zb, 1, 0))
        prev_ref[b] = zb[rows - 1:rows, :]
        zs.append(zb + (prev - zb) * mu)
    zf = jnp.stack(zs, axis=0).reshape(g * c, RWKV_IN)

    pv = pv_ref[...]
    w0, a0, k_k, k_a, r_k, ln_w, ln_b = (pv[i:i + 1] for i in range(7))
    r = zf[:, 0:kd]
    k = zf[:, kd:2 * kd]
    v = zf[:, 2 * kd:3 * kd]
    lo = zf[:, 3 * kd:]
    lane = lax.broadcasted_iota(jnp.int32, (1, kd), 1)
    lora_in = jnp.where(lane < 64, jnp.tanh(lo), jnp.where(lane < 128, lo, _sigmoid(lo)))
    lin = _dg(lora_in.astype(BF16), wl_ref[...], NN)
    lw = -jnp.exp(-_softplus(-(w0 + lin[:, 0:kd])) - 0.5)
    a = _sigmoid(a0 + lin[:, kd:2 * kd])
    gate = lin[:, 2 * kd:]

    hrow = lax.broadcasted_iota(jnp.int32, (kd, kd), 0) // HEAD_DIM
    hcol = lax.broadcasted_iota(jnp.int32, (kd, kd), 1) // HEAD_DIM
    same_head = hrow == hcol
    ones_bd = same_head.astype(BF16)
    head_sum = lambda t: _mm_exact_rhs(t, ones_bd, NN)

    kkr = k * k_k
    kk = kkr / jnp.maximum(jnp.sqrt(head_sum(kkr * kkr)), 1e-12)
    kp = k * (1.0 + (a - 1.0) * k_a)
    bonus = head_sum(r * kp * r_k) * v

    to3 = lambda t: t.reshape(g, c, kd)
    r3, kp3, v3, kk3, a3, lw3 = (to3(t) for t in (r, kp, v, kk, a, lw))

    lower = (lax.broadcasted_iota(jnp.int32, (c, c), 0) >= lax.broadcasted_iota(jnp.int32, (c, c), 1))
    tri = jnp.broadcast_to(lower.astype(BF16)[None], (g, c, c))
    cum = _mm_exact_lhs(tri, lw3, BNN)
    cum_end = cum[:, c - 1:c, :]
    e_in = jnp.exp(cum)
    e_out = jnp.exp(-cum)
    e_tail = jnp.exp(cum_end - cum)
    p_end = jnp.exp(cum_end)
    beta = kk3 * a3
    ab = -kk3 * jnp.exp(cum - lw3)
    rb = r3 * e_in
    bb = beta * e_out
    kb = kp3 * e_out
    bt = beta * e_tail
    kt = kp3 * e_tail

    ri = lax.broadcasted_iota(jnp.int32, (c, kd), 0)
    cj = lax.broadcasted_iota(jnp.int32, (c, kd), 1) % c
    strict = ri > cj
    incl = ri >= cj
    sab, srb = _mm_heads([ab, rb], bb, ones_bd, p_score, BNT)
    sak, srk = _mm_heads([ab, rb], kb, ones_bd, p_score, BNT)
    a_ab = jnp.where(strict, sab, 0.0)
    a_rb = jnp.where(incl, srb, 0.0)
    a_ak = jnp.where(strict, sak, 0.0)
    a_rk = jnp.where(incl, srk, 0.0)
    t_inv = _unit_lower_inverse(a_ab, ones_bd, p_inv)

    app = lambda us, w: _mm_heads(us, w, ones_bd, p_apply, BNN)
    akv, arkv = app([a_ak, a_rk], v3)
    (w_all,) = app([t_inv], ab)
    (u0_all,) = app([t_inv], akv)
    rq = rb + app([a_rb], w_all)[0]
    yc = app([a_rb], u0_all)[0] + arkv

    bt_t = jnp.swapaxes(bt, 1, 2)
    kt_t = jnp.swapaxes(kt, 1, 2)
    eye = (lax.broadcasted_iota(jnp.int32, (kd, kd), 0)
           == lax.broadcasted_iota(jnp.int32, (kd, kd), 1))
    full = lambda u, w: _mm(u, w, BNN, p_apply)
    trans = jnp.where(same_head, full(bt_t, w_all), 0.0) + jnp.where(eye, p_end, 0.0)
    inject = jnp.where(same_head, full(bt_t, u0_all) + full(kt_t, v3), 0.0)

    per_chunk = lambda t, j: t.reshape((nb, nc) + t.shape[1:])[:, j]
    st = st_ref[...]
    ys = []
    for j in range(nc):
        y_st, st_new = _mm_shared([per_chunk(rq, j), per_chunk(trans, j)], st, BNN, p_state)
        ys.append(y_st + per_chunk(yc, j))
        st = st_new + per_chunk(inject, j)
    st_ref[...] = st
    y = jnp.stack(ys, axis=1)

    y2 = y.reshape(g * c, kd)
    mean = head_sum(y2) * (1.0 / HEAD_DIM)
    dv = y2 - mean
    var = head_sum(dv * dv) * (1.0 / HEAD_DIM)
    out = (dv * lax.rsqrt(var + RWKV_GN_EPS) * ln_w + ln_b + bonus) * gate
    y_ref[...] = out.reshape(nb, rows, kd).astype(y_ref.dtype)


def rwkv_mixer(z3, mu, pvec, w_lora, *, passes=(3, 3, 3, 3), chunks_per_step=2):
    b, s, _ = z3.shape
    c = RWKV_CHUNK * chunks_per_step
    return pl.pallas_call(
        functools.partial(_rwkv_kernel, nb=b, nc=chunks_per_step, passes=passes),
        grid=(s // c,),
        in_specs=[
            pl.BlockSpec((b, c, RWKV_IN), lambda t: (0, t, 0)),
            pl.BlockSpec((1, RWKV_IN), lambda t: (0, 0)),
            pl.BlockSpec((8, RWKV_DIM), lambda t: (0, 0)),
            pl.BlockSpec((RWKV_LORA_IN, 3 * RWKV_DIM), lambda t: (0, 0)),
        ],
        out_specs=pl.BlockSpec((b, c, RWKV_DIM), lambda t: (0, t, 0)),
        out_shape=jax.ShapeDtypeStruct((b, s, RWKV_DIM), BF16),
        scratch_shapes=[
            pltpu.VMEM((b, 1, RWKV_IN), F32),
            pltpu.VMEM((b, RWKV_DIM, RWKV_DIM), F32),
        ],
        compiler_params=_params("arbitrary"),
        name="rwkv_mixer",
    )(z3, mu, pvec, w_lora)


def _pool_kernel(z_ref, w_ref, sc_ref, y_ref, ext_ref, *, tp):
    t = pl.program_id(1)
    hl = POOL_HALO
    levels = len(POOL_WINDOWS)

    @pl.when(t == 0)
    def _():
        ext_ref[:, 0:hl, :] = jnp.zeros((levels, hl, POOL_DIM), F32)

    z = z_ref[0]
    sums = []
    cur = z
    for lvl, w in enumerate(POOL_WINDOWS):
        ext_ref[lvl, hl:hl + tp, :] = cur
        back = w // 2
        cur = cur + ext_ref[lvl, hl - back:hl - back + tp, :]
        sums.append(cur)
    for lvl in range(levels):
        ext_ref[lvl, 0:hl, :] = ext_ref[lvl, tp:tp + hl, :]

    lane = lax.broadcasted_iota(jnp.int32, (tp, POOL_DIM), 1)
    pos = (t * tp + lax.broadcasted_iota(jnp.int32, (tp, POOL_DIM), 0) + 1).astype(F32)
    grp = POOL_DIM // levels
    win_sum = sums[-1]
    count = jnp.minimum(pos, float(POOL_WINDOWS[-1]))
    for lvl in range(levels - 2, -1, -1):
        in_grp = lane < (lvl + 1) * grp
        win_sum = jnp.where(in_grp, sums[lvl], win_sum)
        count = jnp.where(in_grp, jnp.minimum(pos, float(POOL_WINDOWS[lvl])), count)
    pooled = win_sum / count - z
    y = _dg(pooled.astype(BF16), w_ref[...], NN) * sc_ref[...]
    y_ref[0] = y.astype(y_ref.dtype)


def pool_mixer(zp3, w_bd, scale, *, tp):
    b, s, _ = zp3.shape
    return pl.pallas_call(
        functools.partial(_pool_kernel, tp=tp),
        grid=(b, s // tp),
        in_specs=[
            pl.BlockSpec((1, tp, POOL_DIM), lambda i, t: (i, t, 0)),
            pl.BlockSpec((POOL_DIM, POOL_DIM), lambda i, t: (0, 0)),
            pl.BlockSpec((1, POOL_DIM), lambda i, t: (0, 0)),
        ],
        out_specs=pl.BlockSpec((1, tp, POOL_DIM), lambda i, t: (i, t, 0)),
        out_shape=jax.ShapeDtypeStruct((b, s, POOL_DIM), BF16),
        scratch_shapes=[pltpu.VMEM((len(POOL_WINDOWS), POOL_HALO + tp, POOL_DIM), F32)],
        compiler_params=_params("arbitrary", "arbitrary"),
        name="pool_mixer",
    )(zp3, w_bd, scale)


def _fox_kernel(q_ref, k_ref, vt_ref, cc_ref, o_ref, ckb_ref, m_ref, l_ref, acc_ref, *, tq, nblk):
    hp = pl.program_id(1)
    i = pl.program_id(2)
    lane = lax.broadcasted_iota(jnp.int32, (1, LANES), 1)
    reps = tq // LANES

    @pl.when(i == 0)
    def _():
        def fill(jb, carry):
            start = pl.multiple_of(jb * tq, tq)
            blk = cc_ref[0, pl.ds(start, tq), :]
            for hh in range(2):
                col = jnp.sum(jnp.where(lane == 2 * hp + hh, blk, 0.0), axis=-1, keepdims=True)
                ckb_ref[hh, pl.ds(start, tq), :] = jnp.broadcast_to(col, (tq, LANES))
            return carry

        lax.fori_loop(0, nblk, fill, 0)

    q2 = q_ref[0]
    m_ref[...] = jnp.full(m_ref.shape, NEG_BIG, F32)
    l_ref[...] = jnp.zeros(l_ref.shape, F32)
    acc_ref[...] = jnp.zeros(acc_ref.shape, F32)
    qh = [jnp.where(lane // HEAD_DIM == hh, q2, jnp.zeros_like(q2)) for hh in range(2)]

    def step(j, masked):
        start = pl.multiple_of(j * tq, tq)
        k2 = k_ref[0, pl.ds(start, tq), :]
        for hh in range(2):
            s = _dg(k2, qh[hh], NT)
            ck = ckb_ref[hh, pl.ds(start, tq), :]
            s = s - jnp.concatenate([ck] * reps, axis=1)
            if masked:
                kpos = lax.broadcasted_iota(jnp.int32, (tq, tq), 0)
                qpos = lax.broadcasted_iota(jnp.int32, (tq, tq), 1)
                s = jnp.where(kpos <= qpos, s, NEG_BIG)
            m_old = m_ref[hh][0:1, :]
            m_new = jnp.maximum(m_old, jnp.max(s, axis=0, keepdims=True))
            alpha = jnp.exp(m_old - m_new)
            p = jnp.exp(s - m_new)
            l_new = alpha * l_ref[hh][0:1, :] + jnp.sum(p, axis=0, keepdims=True)
            vth = vt_ref[0, j, hh * HEAD_DIM:(hh + 1) * HEAD_DIM, :]
            acc_ref[hh] = alpha * acc_ref[hh] + _dg(vth, p.astype(BF16), NN)
            l_ref[hh] = jnp.broadcast_to(l_new, (8, tq))
            m_ref[hh] = jnp.broadcast_to(m_new, (8, tq))

    def body(j, carry):
        step(j, False)
        return carry

    lax.fori_loop(0, i, body, 0)
    step(i, True)
    ot = jnp.concatenate([acc_ref[hh] / l_ref[hh][0:1, :] for hh in range(2)], axis=0)
    o_ref[0] = ot.T.astype(o_ref.dtype)


def fox_attention(q3, k3, vt4, cumc3, *, tq):
    b, s, _ = q3.shape
    nq = s // tq
    pairs = FOX_HEADS // 2
    return pl.pallas_call(
        functools.partial(_fox_kernel, tq=tq, nblk=nq),
        grid=(b, pairs, nq),
        in_specs=[
            pl.BlockSpec((1, tq, LANES), lambda bi, hp, i: (bi, i, hp)),
            pl.BlockSpec((1, s, LANES), lambda bi, hp, i: (bi, 0, hp)),
            pl.BlockSpec((1, nq, LANES, tq), lambda bi, hp, i: (bi, 0, hp, 0)),
            pl.BlockSpec((1, s, LANES), lambda bi, hp, i: (bi, 0, 0)),
        ],
        out_specs=pl.BlockSpec((1, tq, LANES), lambda bi, hp, i: (bi, i, hp)),
        out_shape=jax.ShapeDtypeStruct((b, s, FOX_DIM), BF16),
        scratch_shapes=[
            pltpu.VMEM((2, s, LANES), F32),
            pltpu.VMEM((2, 8, tq), F32),
            pltpu.VMEM((2, 8, tq), F32),
            pltpu.VMEM((2, HEAD_DIM, tq), F32),
        ],
        compiler_params=_params("arbitrary", "arbitrary", "arbitrary"),
        name="fox_attention",
    )(q3, k3, vt4, cumc3)


def _outproj_kernel(yr_ref, yp_ref, yf_ref, x_ref, mod_ref, wo_ref, n2_ref, rw_ref, rb_ref,
                    x1_ref, h2_ref, ti_ref, tg_ref):
    mod = mod_ref[0]
    mixed = (_dg(yr_ref[...], wo_ref[0:RWKV_DIM, :], NN)
             + _dg(yp_ref[...], wo_ref[RWKV_DIM:RWKV_DIM + POOL_DIM, :], NN)
             + _dg(yf_ref[...], wo_ref[RWKV_DIM + POOL_DIM:, :], NN))
    x1 = x_ref[...] + mod[2:3] * mixed
    x1_ref[...] = x1
    h2 = _rms_modulate(x1, n2_ref[...], mod[4:5], mod[3:4])
    h2_ref[...] = h2

    tm = x1.shape[0]
    lane = lax.broadcasted_iota(jnp.int32, (tm, LANES), 1)
    lane_f = lane.astype(F32)
    logits = _mm(h2, rw_ref[...], NN, passes=3) + rb_ref[...]
    work = jnp.where(lane < N_EXPERTS, logits, NEG_BIG)
    top_i = jnp.zeros((tm, LANES), F32)
    top_e = jnp.zeros((tm, LANES), F32)
    v0 = None
    for kk in range(TOP_K):
        vmax = jnp.max(work, axis=-1, keepdims=True)
        idx = jnp.min(jnp.where(work == vmax, lane_f, float(LANES)), axis=-1, keepdims=True)
        if kk == 0:
            v0 = vmax
        top_i = jnp.where(lane == kk, idx, top_i)
        top_e = jnp.where(lane == kk, jnp.exp(vmax - v0), top_e)
        work = jnp.where(lane_f == idx, NEG_BIG, work)
    ti_ref[...] = top_i.astype(jnp.int32)
    tg_ref[...] = top_e / jnp.sum(top_e, axis=-1, keepdims=True)


def out_projection(yr, yp, yf, x2, mod, w_out, norm2, rw, rb, *, seq, tm):
    n, d = x2.shape
    tpb = seq // tm
    row = lambda i: (i, 0)
    const = lambda i: (0, 0)
    return pl.pallas_call(
        _outproj_kernel,
        grid=(n // tm,),
        in_specs=[
            pl.BlockSpec((tm, RWKV_DIM), row),
            pl.BlockSpec((tm, POOL_DIM), row),
            pl.BlockSpec((tm, FOX_DIM), row),
            pl.BlockSpec((tm, d), row),
            pl.BlockSpec((1, 6, d), lambda i: (i // tpb, 0, 0)),
            pl.BlockSpec((d, d), const),
            pl.BlockSpec((1, d), const),
            pl.BlockSpec((d, LANES), const),
            pl.BlockSpec((1, LANES), const),
        ],
        out_specs=(
            pl.BlockSpec((tm, d), row),
            pl.BlockSpec((tm, d), row),
            pl.BlockSpec((tm, LANES), row),
            pl.BlockSpec((tm, LANES), row),
        ),
        out_shape=(
            jax.ShapeDtypeStruct((n, d), F32),
            jax.ShapeDtypeStruct((n, d), F32),
            jax.ShapeDtypeStruct((n, LANES), jnp.int32),
            jax.ShapeDtypeStruct((n, LANES), F32),
        ),
        compiler_params=_params("arbitrary"),
        name="out_projection",
    )(yr, yp, yf, x2, mod, w_out, norm2, rw, rb)


def _rank_kernel(ti_ref, rank_ref, cnt_ref, carry_ref, *, tr):
    @pl.when(pl.program_id(0) == 0)
    def _():
        carry_ref[...] = jnp.zeros_like(carry_ref)

    ti = ti_ref[...]
    lane = lax.broadcasted_iota(jnp.int32, (tr, LANES), 1)
    hits = [ti[:, kk:kk + 1] == lane for kk in range(TOP_K)]
    cnt = hits[0].astype(F32)
    for kk in range(1, TOP_K):
        cnt = cnt + hits[kk].astype(F32)
    before = (lax.broadcasted_iota(jnp.int32, (tr, tr), 0)
              > lax.broadcasted_iota(jnp.int32, (tr, tr), 1)).astype(BF16)
    pre = _dg(before, cnt.astype(BF16), NN) + carry_ref[...]
    rank = jnp.zeros((tr, LANES), F32)
    for kk in range(TOP_K):
        rk = jnp.sum(jnp.where(hits[kk], pre, 0.0), axis=-1, keepdims=True)
        rank = jnp.where(lane == kk, rk, rank)
    rank_ref[...] = rank.astype(jnp.int32)
    total = carry_ref[...] + jnp.sum(cnt, axis=0, keepdims=True)
    carry_ref[...] = total
    cnt_ref[...] = jnp.broadcast_to(total, cnt_ref.shape)


def moe_rank(top_i, *, tr):
    n = top_i.shape[0]
    return pl.pallas_call(
        functools.partial(_rank_kernel, tr=tr),
        grid=(n // tr,),
        in_specs=[pl.BlockSpec((tr, LANES), lambda i: (i, 0))],
        out_specs=(pl.BlockSpec((tr, LANES), lambda i: (i, 0)),
                   pl.BlockSpec((8, LANES), lambda i: (0, 0))),
        out_shape=(jax.ShapeDtypeStruct((n, LANES), jnp.int32),
                   jax.ShapeDtypeStruct((8, LANES), F32)),
        scratch_shapes=[pltpu.VMEM((1, LANES), F32)],
        compiler_params=_params("arbitrary"),
        name="moe_rank",
    )(top_i)


def _routing_tables(top_i, *, rows_per_block, tr):
    n = top_i.shape[0]
    r = rows_per_block
    rank, cnt = moe_rank(top_i, tr=tr)
    counts = cnt[0, :N_EXPERTS].astype(jnp.int32)
    padded = (counts + r - 1) // r * r
    pend = jnp.cumsum(padded)
    pstart = pend - padded
    e_sel = top_i[:, :TOP_K]
    onehot = e_sel[:, :, None] == jnp.arange(N_EXPERTS, dtype=jnp.int32)[None, None, :]
    dest = (jnp.sum(jnp.where(onehot, pstart[None, None, :], 0), axis=-1) + rank[:, :TOP_K]).astype(jnp.int32)
    nb = -(-(n * TOP_K + N_EXPERTS * (r - 1)) // r)
    bstart = jnp.arange(nb, dtype=jnp.int32) * r
    block_e = jnp.minimum(jnp.searchsorted(pend, bstart, side='right'), N_EXPERTS - 1).astype(jnp.int32)
    nvalid = jnp.clip(pstart[block_e] + counts[block_e] - bstart, 0, r).astype(jnp.int32)
    nused = (pend[-1] // r).astype(jnp.int32).reshape(1)
    return dest, block_e, nvalid, nused, nb


def _dispatch_kernel(dest_ref, h_ref, xs_in_ref, xs_ref, sem, *, td):
    del xs_in_ref
    def issue(t, carry):
        for kk in range(TOP_K):
            pltpu.make_async_copy(h_ref.at[pl.ds(t, 1)],
                                  xs_ref.at[pl.ds(dest_ref[0, 0, TOP_K * t + kk], 1)], sem
                                  ).start(priority=kk % 2)
        return carry

    lax.fori_loop(0, td, issue, 0, unroll=8)
    for kk in range(TOP_K):
        pltpu.make_async_copy(h_ref, xs_ref.at[pl.ds(0, td)], sem).wait()


def moe_dispatch(h2, dest2, *, rows, td):
    n, d = h2.shape
    return pl.pallas_call(
        functools.partial(_dispatch_kernel, td=td),
        grid=(n // td,),
        in_specs=[
            pl.BlockSpec((1, 1, TOP_K * td), lambda i: (i, 0, 0), memory_space=pltpu.SMEM),
            pl.BlockSpec((td, d), lambda i: (i, 0)),
            pl.BlockSpec(memory_space=pl.ANY),
        ],
        out_specs=pl.BlockSpec(memory_space=pl.ANY),
        out_shape=jax.ShapeDtypeStruct((rows, d), F32),
        scratch_shapes=[pltpu.SemaphoreType.DMA],
        input_output_aliases={2: 0},
        compiler_params=_params("arbitrary"),
        name="moe_dispatch",
    )(dest2, h2, jnp.zeros((rows, d), F32))


def _expert_kernel(be_ref, nv_ref, nu_ref, x_ref, wgu_ref, bgu_ref, wd_ref, bd_ref, y_ref,
                   wgu_bf, wd_bf, cur_ref):
    b = pl.program_id(0)
    e = be_ref[b]
    nv = nv_ref[b]

    @pl.when(b == 0)
    def _():
        cur_ref[0] = -1

    @pl.when((nv > 0) & (cur_ref[0] != e))
    def _():
        cur_ref[0] = e
        step = 128

        def cast(i, carry):
            s = pl.multiple_of(i * step, step)
            wgu_bf[pl.ds(s, step), :] = wgu_ref[0, 0, pl.ds(s, step), :].astype(BF16)
            wd_bf[pl.ds(s, step), :] = wd_ref[0, 0, pl.ds(s, step), :].astype(BF16)
            return carry

        lax.fori_loop(0, D_MODEL // step, cast, 0)

    @pl.when(nv == 0)
    def _():
        y_ref[...] = jnp.zeros_like(y_ref)

    @pl.when(nv > 0)
    def _():
        x = x_ref[...].astype(BF16)
        gu = _dg(x, wgu_bf[...], NN) + bgu_ref[0, 0]
        gate = jnp.minimum(gu[:, :D_FF], SWIGLU_LIMIT)
        up = jnp.clip(gu[:, D_FF:], -SWIGLU_LIMIT, SWIGLU_LIMIT)
        act = (up + 1.0) * (gate * _sigmoid(SWIGLU_ALPHA * gate))
        y_ref[...] = _dg(act.astype(BF16), wd_bf[...], NN) + bd_ref[0, 0]


def moe_experts(xs, block_e, nvalid, nused, w_gu, b_gu, w_down, b_down, *, layer, r):
    rows, d = xs.shape
    nb = rows // r
    blk = lambda b, be, nv, nu: (jnp.minimum(b, nu[0] - 1), 0)
    wsel = lambda b, be, nv, nu: (layer, be[b], 0, 0)
    grid_spec = pltpu.PrefetchScalarGridSpec(
        num_scalar_prefetch=3,
        grid=(nb,),
        in_specs=[
            pl.BlockSpec((r, d), blk),
            pl.BlockSpec((1, 1, d, 2 * D_FF), wsel),
            pl.BlockSpec((1, 1, 1, 2 * D_FF), wsel),
            pl.BlockSpec((1, 1, D_FF, d), wsel),
            pl.BlockSpec((1, 1, 1, d), wsel),
        ],
        out_specs=pl.BlockSpec((r, d), lambda b, be, nv, nu: (b, 0)),
        scratch_shapes=[
            pltpu.VMEM((d, 2 * D_FF), BF16),
            pltpu.VMEM((D_FF, d), BF16),
            pltpu.SMEM((1,), jnp.int32),
        ],
    )
    return pl.pallas_call(
        _expert_kernel,
        grid_spec=grid_spec,
        out_shape=jax.ShapeDtypeStruct((rows, d), F32),
        compiler_params=_params("arbitrary"),
        name="moe_experts",
    )(block_e, nvalid, nused, xs, w_gu, b_gu, w_down, b_down)


def _combine_kernel(dest_ref, yb_ref, tg_ref, x_ref, mod_ref, nf_ref, o_ref, buf, sem, *, tc, final_norm):
    def issue(t, carry):
        for kk in range(TOP_K):
            pltpu.make_async_copy(yb_ref.at[pl.ds(dest_ref[0, 0, TOP_K * t + kk], 1)],
                                  buf.at[kk, pl.ds(t, 1)], sem).start(priority=kk % 2)
        return carry

    lax.fori_loop(0, tc, issue, 0, unroll=8)
    for kk in range(TOP_K):
        pltpu.make_async_copy(yb_ref.at[pl.ds(0, tc)], buf.at[kk], sem).wait()

    tg = tg_ref[...]
    y = tg[:, 0:1] * buf[0]
    for kk in range(1, TOP_K):
        y = y + tg[:, kk:kk + 1] * buf[kk]
    x = x_ref[...] + mod_ref[0][5:6] * y
    if final_norm:
        ms = jnp.mean(x * x, axis=-1, keepdims=True)
        x = x * lax.rsqrt(ms + NORM_EPS) * nf_ref[...]
    o_ref[...] = x


def moe_combine(yb, dest2, tg, x1, mod, norm_f, *, seq, tc, final_norm):
    n, d = x1.shape
    tpb = seq // tc
    row = lambda i: (i, 0)
    return pl.pallas_call(
        functools.partial(_combine_kernel, tc=tc, final_norm=final_norm),
        grid=(n // tc,),
        in_specs=[
            pl.BlockSpec((1, 1, TOP_K * tc), lambda i: (i, 0, 0), memory_space=pltpu.SMEM),
            pl.BlockSpec(memory_space=pl.ANY),
            pl.BlockSpec((tc, LANES), row),
            pl.BlockSpec((tc, d), row),
            pl.BlockSpec((1, 6, d), lambda i: (i // tpb, 0, 0)),
            pl.BlockSpec((1, d), lambda i: (0, 0)),
        ],
        out_specs=pl.BlockSpec((tc, d), row),
        out_shape=jax.ShapeDtypeStruct((n, d), F32),
        scratch_shapes=[pltpu.VMEM((TOP_K, tc, d), F32), pltpu.SemaphoreType.DMA],
        compiler_params=_params("arbitrary"),
        name="moe_combine",
    )(dest2, yb, tg, x1, mod, norm_f)


def _block_diag(blocks):
    g, m, _ = blocks.shape
    out = jnp.zeros((g * m, g * m), blocks.dtype)
    for i in range(g):
        out = out.at[i * m:(i + 1) * m, i * m:(i + 1) * m].set(blocks[i])
    return out


def _pad_cols(w, cols):
    return jnp.zeros(w.shape[:-1] + (cols,), w.dtype).at[..., :w.shape[-1]].set(w)


def trunk(x, c, ada_w, ada_b, norm1, w_in, w_out, rwkv_mu, rwkv_w0, rwkv_w_up, rwkv_a0, rwkv_a_up,
          rwkv_g_up, rwkv_k_k, rwkv_k_a, rwkv_r_k, rwkv_ln_w, rwkv_ln_b, pool_w, pool_scale, fox_b_f,
          norm2, router_w, router_b, moe_w_gu, moe_b_gu, moe_w_down, moe_b_down, norm_f,
          *, tm=512, tp=512, tq=512, expert_rows=256, td=512, tc=256, rwkv_passes=(1, 3, 1, 3)):
    batch, seq, d = x.shape
    assert tm == tq, "the attention kernel reads the forget sums in the in-projection's time tiling"
    depth = ada_w.shape[0]
    n = batch * seq
    mod_all = ada_modulation(c, ada_w, ada_b)
    x2 = x.reshape(n, d)
    nf = norm_f.reshape(1, d)
    b_gu4 = moe_b_gu.reshape(depth, N_EXPERTS, 1, 2 * D_FF)
    b_down4 = moe_b_down.reshape(depth, N_EXPERTS, 1, d)
    for l in range(depth):
        mod = mod_all[l]
        w_main = w_in[l][:, :MAIN_COLS].astype(BF16)
        w_f = _pad_cols(w_in[l][:, MAIN_COLS:], LANES)
        b_f = _pad_cols(fox_b_f[l].reshape(1, FOX_HEADS), LANES)
        w_lora = jnp.zeros((RWKV_LORA_IN, 3 * RWKV_DIM), F32)
        w_lora = w_lora.at[0:64, 0:RWKV_DIM].set(rwkv_w_up[l])
        w_lora = w_lora.at[64:128, RWKV_DIM:2 * RWKV_DIM].set(rwkv_a_up[l])
        w_lora = w_lora.at[128:256, 2 * RWKV_DIM:].set(rwkv_g_up[l]).astype(BF16)
        pvec = jnp.stack([rwkv_w0[l], rwkv_a0[l], rwkv_k_k[l], rwkv_k_a[l], rwkv_r_k[l].reshape(-1),
                          rwkv_ln_w[l], rwkv_ln_b[l], jnp.zeros((RWKV_DIM,), F32)], axis=0)
        w_pool = _block_diag(pool_w[l]).astype(BF16)
        rw = _pad_cols(router_w[l], LANES)
        rb = _pad_cols(router_b[l].reshape(1, N_EXPERTS), LANES)

        zr, zp, q, k, vt, cumc = in_projection(
            x2, mod, norm1[l].reshape(1, d), w_main, w_f, b_f, batch=batch, seq=seq, tm=tm)
        yr = rwkv_mixer(zr.reshape(batch, seq, RWKV_IN), rwkv_mu[l].reshape(1, RWKV_IN), pvec, w_lora,
                        passes=rwkv_passes)
        yp = pool_mixer(zp.reshape(batch, seq, POOL_DIM), w_pool, pool_scale[l].reshape(1, POOL_DIM), tp=tp)
        to3 = lambda t: t.reshape(batch, seq, t.shape[-1])
        yf = fox_attention(to3(q), to3(k), vt, to3(cumc), tq=tq)
        x1, h2, top_i, top_g = out_projection(
            yr.reshape(n, RWKV_DIM), yp.reshape(n, POOL_DIM), yf.reshape(n, FOX_DIM), x2, mod,
            w_out[l].astype(BF16), norm2[l].reshape(1, d), rw, rb, seq=seq, tm=tm)

        dest, block_e, nvalid, nused, nb = _routing_tables(top_i, rows_per_block=expert_rows, tr=tm)
        xs = moe_dispatch(h2, dest.reshape(n // td, 1, TOP_K * td), rows=nb * expert_rows, td=td)
        yb = moe_experts(xs, block_e, nvalid, nused, moe_w_gu, b_gu4, moe_w_down, b_down4,
                         layer=l, r=expert_rows)
        x2 = moe_combine(yb, dest.reshape(n // tc, 1, TOP_K * tc), top_g, x1, mod, nf, seq=seq, tc=tc,
                         final_norm=(l == depth - 1))
    return x2.reshape(batch, seq, d)


def kernel(x, c, ada_w, ada_b, norm1, w_in, w_out, rwkv_mu, rwkv_w0, rwkv_w_up, rwkv_a0, rwkv_a_up, rwkv_g_up, rwkv_k_k, rwkv_k_a, rwkv_r_k, rwkv_ln_w, rwkv_ln_b, pool_w, pool_scale, fox_b_f, norm2, router_w, router_b, moe_w_gu, moe_b_gu, moe_w_down, moe_b_down, norm_f):
    return trunk(x, c, ada_w, ada_b, norm1, w_in, w_out, rwkv_mu, rwkv_w0, rwkv_w_up, rwkv_a0, rwkv_a_up,
                 rwkv_g_up, rwkv_k_k, rwkv_k_a, rwkv_r_k, rwkv_ln_w, rwkv_ln_b, pool_w, pool_scale, fox_b_f,
                 norm2, router_w, router_b, moe_w_gu, moe_b_gu, moe_w_down, moe_b_down, norm_f)
```

```python
import functools

import jax
import jax.numpy as jnp
from jax import lax
from jax.experimental import pallas as pl
from jax.experimental.pallas import tpu as pltpu

F32 = jnp.float32
BF16 = jnp.bfloat16

D_MODEL = 1024
HEAD_DIM = 64
RWKV_DIM = 256
RWKV_HEADS = 4
RWKV_LORA_IN = 256
RWKV_GN_EPS = 64e-5
POOL_DIM = 256
POOL_WINDOWS = (2, 4, 8, 16)
POOL_HALO = 16
FOX_DIM = 512
FOX_HEADS = 8
N_EXPERTS = 32
TOP_K = 4
D_FF = 1024
SWIGLU_LIMIT = 7.0
SWIGLU_ALPHA = 1.702
NORM_EPS = 1e-6
RWKV_IN = 1024
MAIN_COLS = RWKV_IN + POOL_DIM + 3 * FOX_DIM
LANES = 128
RWKV_CHUNK = 64
NEG_BIG = -1e30
VMEM_LIMIT = 48 * 1024 * 1024
NN = (((1,), (0,)), ((), ()))
NT = (((1,), (1,)), ((), ()))
BNN = (((2,), (1,)), ((0,), (0,)))
BNT = (((2,), (2,)), ((0,), (0,)))


def _dg(a, b, dims):
    return lax.dot_general(a, b, dims, preferred_element_type=F32)


def _split2(x):
    hi = x.astype(BF16)
    lo = (x - hi.astype(F32)).astype(BF16)
    return hi, lo


def _split3(x):
    hi = x.astype(BF16)
    r = x - hi.astype(F32)
    mid = r.astype(BF16)
    lo = (r - mid.astype(F32)).astype(BF16)
    return hi, mid, lo


def _mm(a, b, dims, passes=1):
    if passes == 1:
        return _dg(a.astype(BF16), b.astype(BF16), dims)
    ah, al = _split2(a)
    bh, bl = _split2(b)
    return _dg(ah, bh, dims) + (_dg(ah, bl, dims) + _dg(al, bh, dims))


def _mm_exact_lhs(a_bf16, b, dims):
    hi, mid, lo = _split3(b)
    return _dg(a_bf16, hi, dims) + (_dg(a_bf16, mid, dims) + _dg(a_bf16, lo, dims))


def _mm_exact_rhs(a, b_bf16, dims):
    hi, mid, lo = _split3(a)
    return _dg(hi, b_bf16, dims) + (_dg(mid, b_bf16, dims) + _dg(lo, b_bf16, dims))


def _sigmoid(x):
    return 1.0 / (1.0 + jnp.exp(-x))


def _softplus(x):
    return jnp.maximum(x, 0.0) + jnp.log1p(jnp.exp(-jnp.abs(x)))


def _params(*sem):
    return pltpu.CompilerParams(dimension_semantics=sem, vmem_limit_bytes=VMEM_LIMIT)


def _ada_kernel(c_ref, w_ref, b_ref, o_ref):
    c = c_ref[...]
    cond = c * _sigmoid(c)
    o_ref[0] = _mm(cond, w_ref[0], NN, passes=3) + b_ref[0]


def ada_modulation(c, ada_w, ada_b):
    L, D, D6 = ada_w.shape
    B = c.shape[0]
    rows = 8
    cp = jnp.zeros((rows, D), F32).at[:B].set(c)
    out = pl.pallas_call(
        _ada_kernel,
        grid=(L, D6 // D),
        in_specs=[
            pl.BlockSpec((rows, D), lambda l, j: (0, 0)),
            pl.BlockSpec((1, D, D), lambda l, j: (l, 0, j)),
            pl.BlockSpec((1, 1, D), lambda l, j: (l, 0, j)),
        ],
        out_specs=pl.BlockSpec((1, rows, D), lambda l, j: (l, 0, j)),
        out_shape=jax.ShapeDtypeStruct((L, rows, D6), F32),
        compiler_params=_params("arbitrary", "arbitrary"),
        name="ada_modulation",
    )(cp, ada_w, ada_b.reshape(L, 1, D6))
    return out[:, :B].reshape(L, B, D6 // D, D)


def _rms_modulate(x, gain, scale, shift):
    ms = jnp.mean(x * x, axis=-1, keepdims=True)
    return x * lax.rsqrt(ms + NORM_EPS) * gain * (1.0 + scale) + shift


def _inproj_kernel(x_ref, mod_ref, n1_ref, wm_ref, wf_ref, bf_ref,
                   zr_ref, zp_ref, q_ref, k_ref, vt_ref, cumc_ref, carry_ref,
                   *, tiles_per_batch, tm, tv):
    i = pl.program_id(0)

    @pl.when(i % tiles_per_batch == 0)
    def _():
        carry_ref[...] = jnp.zeros_like(carry_ref)

    mod = mod_ref[0]
    h = _rms_modulate(x_ref[...], n1_ref[...], mod[1:2], mod[0:1])
    z = _dg(h.astype(BF16), wm_ref[...], NN)
    zr_ref[...] = z[:, :RWKV_IN]
    zp_ref[...] = z[:, RWKV_IN:RWKV_IN + POOL_DIM]
    o = RWKV_IN + POOL_DIM
    q_ref[...] = (z[:, o:o + FOX_DIM] * (HEAD_DIM ** -0.5)).astype(BF16)
    k_ref[...] = z[:, o + FOX_DIM:o + 2 * FOX_DIM].astype(BF16)
    for u in range(tm // tv):
        vt_ref[0, u] = z[u * tv:(u + 1) * tv, o + 2 * FOX_DIM:o + 3 * FOX_DIM].T.astype(BF16)

    fl = _mm(h, wf_ref[...], NN, passes=3) + bf_ref[...]
    logf = jnp.minimum(fl, 0.0) - jnp.log1p(jnp.exp(-jnp.abs(fl)))
    row = lax.broadcasted_iota(jnp.int32, (tm, tm), 0)
    col = lax.broadcasted_iota(jnp.int32, (tm, tm), 1)
    tri = (row >= col).astype(BF16)
    cum = _mm_exact_lhs(tri, logf, NN) + carry_ref[...]
    carry_ref[...] = cum[tm - 1:tm, :]
    cumc_ref[...] = cum


def in_projection(x2, mod, norm1, w_main, w_f, b_f, *, batch, seq, tm, tv):
    n, d = x2.shape
    tpb = seq // tm
    row = lambda i: (i, 0)
    const = lambda i: (0, 0)
    out_shapes = (
        jax.ShapeDtypeStruct((n, RWKV_IN), F32),
        jax.ShapeDtypeStruct((n, POOL_DIM), F32),
        jax.ShapeDtypeStruct((n, FOX_DIM), BF16),
        jax.ShapeDtypeStruct((n, FOX_DIM), BF16),
        jax.ShapeDtypeStruct((batch, seq // tv, FOX_DIM, tv), BF16),
        jax.ShapeDtypeStruct((n, LANES), F32),
    )
    return pl.pallas_call(
        functools.partial(_inproj_kernel, tiles_per_batch=tpb, tm=tm, tv=tv),
        grid=(n // tm,),
        in_specs=[
            pl.BlockSpec((tm, d), row),
            pl.BlockSpec((1, 6, d), lambda i: (i // tpb, 0, 0)),
            pl.BlockSpec((1, d), const),
            pl.BlockSpec((d, MAIN_COLS), const),
            pl.BlockSpec((d, LANES), const),
            pl.BlockSpec((1, LANES), const),
        ],
        out_specs=(
            pl.BlockSpec((tm, RWKV_IN), row),
            pl.BlockSpec((tm, POOL_DIM), row),
            pl.BlockSpec((tm, FOX_DIM), row),
            pl.BlockSpec((tm, FOX_DIM), row),
            pl.BlockSpec((1, tm // tv, FOX_DIM, tv), lambda i: (i // tpb, i % tpb, 0, 0)),
            pl.BlockSpec((tm, LANES), row),
        ),
        out_shape=out_shapes,
        scratch_shapes=[pltpu.VMEM((1, LANES), F32)],
        compiler_params=_params("arbitrary"),
        name="in_projection",
    )(x2, mod, norm1, w_main, w_f, b_f)


def _head_blocks(y, same_head):
    return jnp.concatenate([y] * RWKV_HEADS, axis=1) * same_head


def _mm_shared(xs, y, dims, passes):
    sizes = [x.shape[1] for x in xs]
    x = xs[0] if len(xs) == 1 else jnp.concatenate(xs, axis=1)
    m = x.shape[1]
    if passes == 1:
        out = _dg(x.astype(BF16), y.astype(BF16), dims)
    else:
        xh, xl = _split2(x)
        yh, yl = y if isinstance(y, tuple) else _split2(y)
        both = _dg(jnp.concatenate([xh, xl], axis=1), yh, dims)
        out = both[:, :m] + (both[:, m:] + _dg(xh, yl, dims))
    outs, start = [], 0
    for s in sizes:
        outs.append(out[:, start:start + s])
        start += s
    return outs


def _mm_heads(xs, y, same_head, passes, dims):
    if passes == 1:
        return _mm_shared(xs, _head_blocks(y.astype(BF16), same_head), dims, 1)
    yh, yl = _split2(y)
    return _mm_shared(xs, (_head_blocks(yh, same_head), _head_blocks(yl, same_head)), dims, passes)


def _unit_lower_inverse(a, same_head, passes):
    c = a.shape[-2]
    ri = lax.broadcasted_iota(jnp.int32, a.shape[-2:], 0)
    ci = lax.broadcasted_iota(jnp.int32, a.shape[-2:], 1) % c
    same16 = (ri // 16) == (ci // 16)
    same32 = (ri // 32) == (ci // 32)
    eye = (ri == ci).astype(F32)
    mm = lambda us, v: _mm_heads(us, v, same_head, passes, BNN)
    a16 = jnp.where(same16, a, 0.0)
    t = eye + a16
    (p,) = mm([a16], a16)
    for _ in range(2):
        tp, p2 = mm([t, p], p)
        t = t + tp
        p = p2
    t = t + mm([t], p)[0]
    o16 = jnp.where(same32 & jnp.logical_not(same16), a, 0.0)
    t = t + mm([t], mm([o16], t)[0])[0]
    o32 = jnp.where(same32, 0.0, a)
    return t + mm([t], mm([o32], t)[0])[0]


def _rwkv_kernel(z_ref, mu_ref, pv_ref, wl_ref, y_ref, prev_ref, st_ref, *, nb, nc, passes):
    c = RWKV_CHUNK
    kd = RWKV_DIM
    g = nb * nc
    rows = nc * c
    p_score, p_inv, p_apply, p_state = passes

    @pl.when(pl.program_id(0) == 0)
    def _():
        prev_ref[...] = jnp.zeros_like(prev_ref)
        st_ref[...] = jnp.zeros_like(st_ref)

    mu = mu_ref[...]
    row0 = lax.broadcasted_iota(jnp.int32, (rows, RWKV_IN), 0) == 0
    zs = []
    for b in range(nb):
        zb = z_ref[b]
        prev = jnp.where(row0, prev_ref[b], pltpu.roll(zb, 1, 0))
        prev_ref[b] = zb[rows - 1:rows, :]
        zs.append(zb + (prev - zb) * mu)
    zf = jnp.stack(zs, axis=0).reshape(g * c, RWKV_IN)

    pv = pv_ref[...]
    w0, a0, k_k, k_a, r_k, ln_w, ln_b = (pv[i:i + 1] for i in range(7))
    r = zf[:, 0:kd]
    k = zf[:, kd:2 * kd]
    v = zf[:, 2 * kd:3 * kd]
    lo = zf[:, 3 * kd:]
    lane = lax.broadcasted_iota(jnp.int32, (1, kd), 1)
    lora_in = jnp.where(lane < 64, jnp.tanh(lo), jnp.where(lane < 128, lo, _sigmoid(lo)))
    lin = _dg(lora_in.astype(BF16), wl_ref[...], NN)
    lw = -jnp.exp(-_softplus(-(w0 + lin[:, 0:kd])) - 0.5)
    a = _sigmoid(a0 + lin[:, kd:2 * kd])
    gate = lin[:, 2 * kd:]

    hrow = lax.broadcasted_iota(jnp.int32, (kd, kd), 0) // HEAD_DIM
    hcol = lax.broadcasted_iota(jnp.int32, (kd, kd), 1) // HEAD_DIM
    same_head = hrow == hcol
    ones_bd = same_head.astype(BF16)
    head_sum = lambda t: _mm_exact_rhs(t, ones_bd, NN)

    kkr = k * k_k
    kk = kkr / jnp.maximum(jnp.sqrt(head_sum(kkr * kkr)), 1e-12)
    kp = k * (1.0 + (a - 1.0) * k_a)
    bonus = head_sum(r * kp * r_k) * v

    to3 = lambda t: t.reshape(g, c, kd)
    r3, kp3, v3, kk3, a3, lw3 = (to3(t) for t in (r, kp, v, kk, a, lw))

    lower = (lax.broadcasted_iota(jnp.int32, (c, c), 0) >= lax.broadcasted_iota(jnp.int32, (c, c), 1))
    tri = jnp.broadcast_to(lower.astype(BF16)[None], (g, c, c))
    cum = _mm_exact_lhs(tri, lw3, BNN)
    cum_end = cum[:, c - 1:c, :]
    e_in = jnp.exp(cum)
    e_out = jnp.exp(-cum)
    e_tail = jnp.exp(cum_end - cum)
    p_end = jnp.exp(cum_end)
    beta = kk3 * a3
    ab = -kk3 * jnp.exp(cum - lw3)
    rb = r3 * e_in
    bb = beta * e_out
    kb = kp3 * e_out
    bt = beta * e_tail
    kt = kp3 * e_tail

    ri = lax.broadcasted_iota(jnp.int32, (c, kd), 0)
    cj = lax.broadcasted_iota(jnp.int32, (c, kd), 1) % c
    strict = ri > cj
    incl = ri >= cj
    sab, srb = _mm_heads([ab, rb], bb, ones_bd, p_score, BNT)
    sak, srk = _mm_heads([ab, rb], kb, ones_bd, p_score, BNT)
    a_ab = jnp.where(strict, sab, 0.0)
    a_rb = jnp.where(incl, srb, 0.0)
    a_ak = jnp.where(strict, sak, 0.0)
    a_rk = jnp.where(incl, srk, 0.0)
    t_inv = _unit_lower_inverse(a_ab, ones_bd, p_inv)

    app = lambda us, w: _mm_heads(us, w, ones_bd, p_apply, BNN)
    akv, arkv = app([a_ak, a_rk], v3)
    (w_all,) = app([t_inv], ab)
    (u0_all,) = app([t_inv], akv)
    rq = rb + app([a_rb], w_all)[0]
    yc = app([a_rb], u0_all)[0] + arkv

    bt_t = jnp.swapaxes(bt, 1, 2)
    kt_t = jnp.swapaxes(kt, 1, 2)
    eye = (lax.broadcasted_iota(jnp.int32, (kd, kd), 0)
           == lax.broadcasted_iota(jnp.int32, (kd, kd), 1))
    full = lambda u, w: _mm(u, w, BNN, p_apply)
    trans = jnp.where(same_head, full(bt_t, w_all), 0.0) + jnp.where(eye, p_end, 0.0)
    inject = jnp.where(same_head, full(bt_t, u0_all) + full(kt_t, v3), 0.0)

    per_chunk = lambda t, j: t.reshape((nb, nc) + t.shape[1:])[:, j]
    st = st_ref[...]
    ys = []
    for j in range(nc):
        y_st, st_new = _mm_shared([per_chunk(rq, j), per_chunk(trans, j)], st, BNN, p_state)
        ys.append(y_st + per_chunk(yc, j))
        st = st_new + per_chunk(inject, j)
    st_ref[...] = st
    y = jnp.stack(ys, axis=1)

    y2 = y.reshape(g * c, kd)
    mean = head_sum(y2) * (1.0 / HEAD_DIM)
    dv = y2 - mean
    var = head_sum(dv * dv) * (1.0 / HEAD_DIM)
    out = (dv * lax.rsqrt(var + RWKV_GN_EPS) * ln_w + ln_b + bonus) * gate
    y_ref[...] = out.reshape(nb, rows, kd).astype(y_ref.dtype)


def rwkv_mixer(z3, mu, pvec, w_lora, *, passes=(3, 3, 3, 3), chunks_per_step=2):
    b, s, _ = z3.shape
    c = RWKV_CHUNK * chunks_per_step
    return pl.pallas_call(
        functools.partial(_rwkv_kernel, nb=b, nc=chunks_per_step, passes=passes),
        grid=(s // c,),
        in_specs=[
            pl.BlockSpec((b, c, RWKV_IN), lambda t: (0, t, 0)),
            pl.BlockSpec((1, RWKV_IN), lambda t: (0, 0)),
            pl.BlockSpec((8, RWKV_DIM), lambda t: (0, 0)),
            pl.BlockSpec((RWKV_LORA_IN, 3 * RWKV_DIM), lambda t: (0, 0)),
        ],
        out_specs=pl.BlockSpec((b, c, RWKV_DIM), lambda t: (0, t, 0)),
        out_shape=jax.ShapeDtypeStruct((b, s, RWKV_DIM), BF16),
        scratch_shapes=[
            pltpu.VMEM((b, 1, RWKV_IN), F32),
            pltpu.VMEM((b, RWKV_DIM, RWKV_DIM), F32),
        ],
        compiler_params=_params("arbitrary"),
        name="rwkv_mixer",
    )(z3, mu, pvec, w_lora)


def _pool_kernel(z_ref, w_ref, sc_ref, y_ref, ext_ref, *, tp):
    t = pl.program_id(1)
    hl = POOL_HALO
    levels = len(POOL_WINDOWS)

    @pl.when(t == 0)
    def _():
        ext_ref[:, 0:hl, :] = jnp.zeros((levels, hl, POOL_DIM), F32)

    z = z_ref[0]
    sums = []
    cur = z
    for lvl, w in enumerate(POOL_WINDOWS):
        ext_ref[lvl, hl:hl + tp, :] = cur
        back = w // 2
        cur = cur + ext_ref[lvl, hl - back:hl - back + tp, :]
        sums.append(cur)
    for lvl in range(levels):
        ext_ref[lvl, 0:hl, :] = ext_ref[lvl, tp:tp + hl, :]

    lane = lax.broadcasted_iota(jnp.int32, (tp, POOL_DIM), 1)
    pos = (t * tp + lax.broadcasted_iota(jnp.int32, (tp, POOL_DIM), 0) + 1).astype(F32)
    grp = POOL_DIM // levels
    win_sum = sums[-1]
    count = jnp.minimum(pos, float(POOL_WINDOWS[-1]))
    for lvl in range(levels - 2, -1, -1):
        in_grp = lane < (lvl + 1) * grp
        win_sum = jnp.where(in_grp, sums[lvl], win_sum)
        count = jnp.where(in_grp, jnp.minimum(pos, float(POOL_WINDOWS[lvl])), count)
    pooled = win_sum / count - z
    y = _dg(pooled.astype(BF16), w_ref[...], NN) * sc_ref[...]
    y_ref[0] = y.astype(y_ref.dtype)


def pool_mixer(zp3, w_bd, scale, *, tp):
    b, s, _ = zp3.shape
    return pl.pallas_call(
        functools.partial(_pool_kernel, tp=tp),
        grid=(b, s // tp),
        in_specs=[
            pl.BlockSpec((1, tp, POOL_DIM), lambda i, t: (i, t, 0)),
            pl.BlockSpec((POOL_DIM, POOL_DIM), lambda i, t: (0, 0)),
            pl.BlockSpec((1, POOL_DIM), lambda i, t: (0, 0)),
        ],
        out_specs=pl.BlockSpec((1, tp, POOL_DIM), lambda i, t: (i, t, 0)),
        out_shape=jax.ShapeDtypeStruct((b, s, POOL_DIM), BF16),
        scratch_shapes=[pltpu.VMEM((len(POOL_WINDOWS), POOL_HALO + tp, POOL_DIM), F32)],
        compiler_params=_params("arbitrary", "arbitrary"),
        name="pool_mixer",
    )(zp3, w_bd, scale)


def _fox_kernel(q_ref, k_ref, vt_ref, cc_ref, o_ref, ckb_ref, m_ref, l_ref, acc_ref, sa_ref, sb_ref,
                *, tq, nblk):
    hp = pl.program_id(1)
    i = pl.program_id(2)
    lane = lax.broadcasted_iota(jnp.int32, (1, LANES), 1)
    reps = tq // LANES

    @pl.when(i == 0)
    def _():
        def fill(jb, carry):
            start = pl.multiple_of(jb * tq, tq)
            blk = cc_ref[0, pl.ds(start, tq), :]
            for hh in range(2):
                col = jnp.sum(jnp.where(lane == 2 * hp + hh, blk, 0.0), axis=-1, keepdims=True)
                ckb_ref[hh, pl.ds(start, tq), :] = jnp.broadcast_to(col, (tq, LANES))
            return carry

        lax.fori_loop(0, nblk, fill, 0)

    q2 = q_ref[0]
    m_ref[...] = jnp.full(m_ref.shape, NEG_BIG, F32)
    l_ref[...] = jnp.zeros(l_ref.shape, F32)
    acc_ref[...] = jnp.zeros(acc_ref.shape, F32)
    qh = [jnp.where(lane // HEAD_DIM == hh, q2, jnp.zeros_like(q2)) for hh in range(2)]

    def scores_into(j, dst):
        k2 = k_ref[0, pl.ds(pl.multiple_of(j * tq, tq), tq), :]
        for hh in range(2):
            dst[hh] = _dg(k2, qh[hh], NT)

    def consume(j, src, masked):
        start = pl.multiple_of(j * tq, tq)
        for hh in range(2):
            ck = ckb_ref[hh, pl.ds(start, tq), :]
            s = src[hh] - jnp.concatenate([ck] * reps, axis=1)
            if masked:
                kpos = lax.broadcasted_iota(jnp.int32, (tq, tq), 0)
                qpos = lax.broadcasted_iota(jnp.int32, (tq, tq), 1)
                s = jnp.where(kpos <= qpos, s, NEG_BIG)
            m_old = m_ref[hh][0:1, :]
            m_new = jnp.maximum(m_old, jnp.max(s, axis=0, keepdims=True))
            alpha = jnp.exp(m_old - m_new)
            p = jnp.exp(s - m_new)
            l_new = alpha * l_ref[hh][0:1, :] + jnp.sum(p, axis=0, keepdims=True)
            vth = vt_ref[0, j, hh * HEAD_DIM:(hh + 1) * HEAD_DIM, :]
            acc_ref[hh] = alpha * acc_ref[hh] + _dg(vth, p.astype(BF16), NN)
            l_ref[hh] = jnp.broadcast_to(l_new, (8, tq))
            m_ref[hh] = jnp.broadcast_to(m_new, (8, tq))

    scores_into(0, sa_ref)

    def pair(jj, carry):
        j = 2 * jj
        scores_into(j + 1, sb_ref)
        consume(j, sa_ref, False)
        scores_into(j + 2, sa_ref)
        consume(j + 1, sb_ref, False)
        return carry

    lax.fori_loop(0, i // 2, pair, 0)

    @pl.when(i % 2 == 0)
    def _():
        consume(i, sa_ref, True)

    @pl.when(i % 2 == 1)
    def _():
        scores_into(i, sb_ref)
        consume(i - 1, sa_ref, False)
        consume(i, sb_ref, True)

    ot = jnp.concatenate([acc_ref[hh] / l_ref[hh][0:1, :] for hh in range(2)], axis=0)
    o_ref[0] = ot.T.astype(o_ref.dtype)


def fox_attention(q3, k3, vt4, cumc3, *, tq):
    b, s, _ = q3.shape
    nq = s // tq
    pairs = FOX_HEADS // 2
    return pl.pallas_call(
        functools.partial(_fox_kernel, tq=tq, nblk=nq),
        grid=(b, pairs, nq),
        in_specs=[
            pl.BlockSpec((1, tq, LANES), lambda bi, hp, i: (bi, i, hp)),
            pl.BlockSpec((1, s, LANES), lambda bi, hp, i: (bi, 0, hp)),
            pl.BlockSpec((1, nq, LANES, tq), lambda bi, hp, i: (bi, 0, hp, 0)),
            pl.BlockSpec((1, s, LANES), lambda bi, hp, i: (bi, 0, 0)),
        ],
        out_specs=pl.BlockSpec((1, tq, LANES), lambda bi, hp, i: (bi, i, hp)),
        out_shape=jax.ShapeDtypeStruct((b, s, FOX_DIM), BF16),
        scratch_shapes=[
            pltpu.VMEM((2, s, LANES), F32),
            pltpu.VMEM((2, 8, tq), F32),
            pltpu.VMEM((2, 8, tq), F32),
            pltpu.VMEM((2, HEAD_DIM, tq), F32),
            pltpu.VMEM((2, tq, tq), F32),
            pltpu.VMEM((2, tq, tq), F32),
        ],
        compiler_params=_params("arbitrary", "arbitrary", "arbitrary"),
        name="fox_attention",
    )(q3, k3, vt4, cumc3)


def _outproj_kernel(yr_ref, yp_ref, yf_ref, x_ref, mod_ref, wo_ref, n2_ref, rw_ref, rb_ref,
                    x1_ref, h2_ref, ti_ref, tg_ref):
    mod = mod_ref[0]
    mixed = (_dg(yr_ref[...], wo_ref[0:RWKV_DIM, :], NN)
             + _dg(yp_ref[...], wo_ref[RWKV_DIM:RWKV_DIM + POOL_DIM, :], NN)
             + _dg(yf_ref[...], wo_ref[RWKV_DIM + POOL_DIM:, :], NN))
    x1 = x_ref[...] + mod[2:3] * mixed
    x1_ref[...] = x1
    h2 = _rms_modulate(x1, n2_ref[...], mod[4:5], mod[3:4])
    h2_ref[...] = h2

    tm = x1.shape[0]
    lane = lax.broadcasted_iota(jnp.int32, (tm, LANES), 1)
    lane_f = lane.astype(F32)
    logits = _mm(h2, rw_ref[...], NN, passes=3) + rb_ref[...]
    work = jnp.where(lane < N_EXPERTS, logits, NEG_BIG)
    top_i = jnp.zeros((tm, LANES), F32)
    top_e = jnp.zeros((tm, LANES), F32)
    v0 = None
    for kk in range(TOP_K):
        vmax = jnp.max(work, axis=-1, keepdims=True)
        idx = jnp.min(jnp.where(work == vmax, lane_f, float(LANES)), axis=-1, keepdims=True)
        if kk == 0:
            v0 = vmax
        top_i = jnp.where(lane == kk, idx, top_i)
        top_e = jnp.where(lane == kk, jnp.exp(vmax - v0), top_e)
        work = jnp.where(lane_f == idx, NEG_BIG, work)
    ti_ref[...] = top_i.astype(jnp.int32)
    tg_ref[...] = top_e / jnp.sum(top_e, axis=-1, keepdims=True)


def out_projection(yr, yp, yf, x2, mod, w_out, norm2, rw, rb, *, seq, tm):
    n, d = x2.shape
    tpb = seq // tm
    row = lambda i: (i, 0)
    const = lambda i: (0, 0)
    return pl.pallas_call(
        _outproj_kernel,
        grid=(n // tm,),
        in_specs=[
            pl.BlockSpec((tm, RWKV_DIM), row),
            pl.BlockSpec((tm, POOL_DIM), row),
            pl.BlockSpec((tm, FOX_DIM), row),
            pl.BlockSpec((tm, d), row),
            pl.BlockSpec((1, 6, d), lambda i: (i // tpb, 0, 0)),
            pl.BlockSpec((d, d), const),
            pl.BlockSpec((1, d), const),
            pl.BlockSpec((d, LANES), const),
            pl.BlockSpec((1, LANES), const),
        ],
        out_specs=(
            pl.BlockSpec((tm, d), row),
            pl.BlockSpec((tm, d), row),
            pl.BlockSpec((tm, LANES), row),
            pl.BlockSpec((tm, LANES), row),
        ),
        out_shape=(
            jax.ShapeDtypeStruct((n, d), F32),
            jax.ShapeDtypeStruct((n, d), F32),
            jax.ShapeDtypeStruct((n, LANES), jnp.int32),
            jax.ShapeDtypeStruct((n, LANES), F32),
        ),
        compiler_params=_params("arbitrary"),
        name="out_projection",
    )(yr, yp, yf, x2, mod, w_out, norm2, rw, rb)


def _rank_kernel(ti_ref, rank_ref, cnt_ref, carry_ref, *, tr):
    @pl.when(pl.program_id(0) == 0)
    def _():
        carry_ref[...] = jnp.zeros_like(carry_ref)

    ti = ti_ref[...]
    lane = lax.broadcasted_iota(jnp.int32, (tr, LANES), 1)
    hits = [ti[:, kk:kk + 1] == lane for kk in range(TOP_K)]
    cnt = hits[0].astype(F32)
    for kk in range(1, TOP_K):
        cnt = cnt + hits[kk].astype(F32)
    before = (lax.broadcasted_iota(jnp.int32, (tr, tr), 0)
              > lax.broadcasted_iota(jnp.int32, (tr, tr), 1)).astype(BF16)
    pre = _dg(before, cnt.astype(BF16), NN) + carry_ref[...]
    rank = jnp.zeros((tr, LANES), F32)
    for kk in range(TOP_K):
        rk = jnp.sum(jnp.where(hits[kk], pre, 0.0), axis=-1, keepdims=True)
        rank = jnp.where(lane == kk, rk, rank)
    rank_ref[...] = rank.astype(jnp.int32)
    total = carry_ref[...] + jnp.sum(cnt, axis=0, keepdims=True)
    carry_ref[...] = total
    cnt_ref[...] = jnp.broadcast_to(total, cnt_ref.shape)


def moe_rank(top_i, *, tr):
    n = top_i.shape[0]
    return pl.pallas_call(
        functools.partial(_rank_kernel, tr=tr),
        grid=(n // tr,),
        in_specs=[pl.BlockSpec((tr, LANES), lambda i: (i, 0))],
        out_specs=(pl.BlockSpec((tr, LANES), lambda i: (i, 0)),
                   pl.BlockSpec((8, LANES), lambda i: (0, 0))),
        out_shape=(jax.ShapeDtypeStruct((n, LANES), jnp.int32),
                   jax.ShapeDtypeStruct((8, LANES), F32)),
        scratch_shapes=[pltpu.VMEM((1, LANES), F32)],
        compiler_params=_params("arbitrary"),
        name="moe_rank",
    )(top_i)


def _routing_tables(top_i, *, rows_per_block, tr):
    n = top_i.shape[0]
    r = rows_per_block
    rank, cnt = moe_rank(top_i, tr=tr)
    counts = cnt[0, :N_EXPERTS].astype(jnp.int32)
    padded = (counts + r - 1) // r * r
    pend = jnp.cumsum(padded)
    pstart = pend - padded
    e_sel = top_i[:, :TOP_K]
    onehot = e_sel[:, :, None] == jnp.arange(N_EXPERTS, dtype=jnp.int32)[None, None, :]
    dest = (jnp.sum(jnp.where(onehot, pstart[None, None, :], 0), axis=-1) + rank[:, :TOP_K]).astype(jnp.int32)
    nb = -(-(n * TOP_K + N_EXPERTS * (r - 1)) // r)
    bstart = jnp.arange(nb, dtype=jnp.int32) * r
    block_e = jnp.minimum(jnp.sum(bstart[:, None] >= pend[None, :], axis=1), N_EXPERTS - 1).astype(jnp.int32)
    in_e = block_e[:, None] == jnp.arange(N_EXPERTS, dtype=jnp.int32)[None, :]
    seg_end = jnp.sum(jnp.where(in_e, (pstart + counts)[None, :], 0), axis=1)
    nvalid = jnp.clip(seg_end - bstart, 0, r).astype(jnp.int32)
    nused = (pend[-1] // r).astype(jnp.int32).reshape(1)
    pad = jnp.stack([pstart + counts, padded - counts]).astype(jnp.int32)
    return dest, block_e, nvalid, nused, pad, nb


def _zero_fill(pad_ref, nu_ref, xs_ref, zero_ref, zsem, *, r, nb):
    row_sem, block_sem = zsem.at[0], zsem.at[1]
    zero_ref[...] = jnp.zeros_like(zero_ref)

    def each_copy(fn):
        def per_expert(e, carry):
            start = pad_ref[0, e]

            def per_row(i, c2):
                fn(pltpu.make_async_copy(zero_ref.at[pl.ds(0, 1)], xs_ref.at[pl.ds(start + i, 1)], row_sem))
                return c2

            lax.fori_loop(0, pad_ref[1, e], per_row, 0)
            return carry

        lax.fori_loop(0, N_EXPERTS, per_expert, 0)

        def per_block(b, carry):
            fn(pltpu.make_async_copy(zero_ref, xs_ref.at[pl.ds(pl.multiple_of(b * r, r), r)], block_sem))
            return carry

        lax.fori_loop(nu_ref[0], nb, per_block, 0)

    each_copy(lambda cp: cp.start())
    each_copy(lambda cp: cp.wait())


def _dispatch_kernel(dest_ref, pad_ref, nu_ref, h_ref, xs_ref, zero_ref, sem, zsem, *, td, r, nb):
    @pl.when(pl.program_id(0) == 0)
    def _():
        _zero_fill(pad_ref, nu_ref, xs_ref, zero_ref, zsem, r=r, nb=nb)

    def issue(t, carry):
        for kk in range(TOP_K):
            pltpu.make_async_copy(h_ref.at[pl.ds(t, 1)],
                                  xs_ref.at[pl.ds(dest_ref[0, 0, TOP_K * t + kk], 1)], sem
                                  ).start(priority=kk % 2)
        return carry

    lax.fori_loop(0, td, issue, 0, unroll=8)
    for kk in range(TOP_K):
        pltpu.make_async_copy(h_ref, xs_ref.at[pl.ds(0, td)], sem).wait()


def moe_dispatch(h2, dest2, pad, nused, *, nb, r, td):
    n, d = h2.shape
    return pl.pallas_call(
        functools.partial(_dispatch_kernel, td=td, r=r, nb=nb),
        grid=(n // td,),
        in_specs=[
            pl.BlockSpec((1, 1, TOP_K * td), lambda i: (i, 0, 0), memory_space=pltpu.SMEM),
            pl.BlockSpec(memory_space=pltpu.SMEM),
            pl.BlockSpec(memory_space=pltpu.SMEM),
            pl.BlockSpec((td, d), lambda i: (i, 0)),
        ],
        out_specs=pl.BlockSpec(memory_space=pl.ANY),
        out_shape=jax.ShapeDtypeStruct((nb * r, d), F32),
        scratch_shapes=[pltpu.VMEM((r, d), F32), pltpu.SemaphoreType.DMA, pltpu.SemaphoreType.DMA((2,))],
        compiler_params=_params("arbitrary"),
        name="moe_dispatch",
    )(dest2, pad, nused, h2)


def _expert_kernel(be_ref, nv_ref, nu_ref, x_ref, wgu_ref, bgu_ref, wd_ref, bd_ref, y_ref,
                   wgu_bf, wd_bf, cur_ref):
    b = pl.program_id(0)
    e = be_ref[b]
    nv = nv_ref[b]

    @pl.when(b == 0)
    def _():
        cur_ref[0] = -1

    @pl.when((nv > 0) & (cur_ref[0] != e))
    def _():
        cur_ref[0] = e
        step = 128

        def cast(i, carry):
            s = pl.multiple_of(i * step, step)
            wgu_bf[pl.ds(s, step), :] = wgu_ref[0, 0, pl.ds(s, step), :].astype(BF16)
            wd_bf[pl.ds(s, step), :] = wd_ref[0, 0, pl.ds(s, step), :].astype(BF16)
            return carry

        lax.fori_loop(0, D_MODEL // step, cast, 0)

    @pl.when(nv == 0)
    def _():
        y_ref[...] = jnp.zeros_like(y_ref)

    @pl.when(nv > 0)
    def _():
        x = x_ref[...].astype(BF16)
        gu = _dg(x, wgu_bf[...], NN) + bgu_ref[0, 0]
        gate = jnp.minimum(gu[:, :D_FF], SWIGLU_LIMIT)
        up = jnp.clip(gu[:, D_FF:], -SWIGLU_LIMIT, SWIGLU_LIMIT)
        act = (up + 1.0) * (gate * _sigmoid(SWIGLU_ALPHA * gate))
        y_ref[...] = _dg(act.astype(BF16), wd_bf[...], NN) + bd_ref[0, 0]


def moe_experts(xs, block_e, nvalid, nused, w_gu, b_gu, w_down, b_down, *, layer, r):
    rows, d = xs.shape
    nb = rows // r
    blk = lambda b, be, nv, nu: (jnp.minimum(b, nu[0] - 1), 0)
    wsel = lambda b, be, nv, nu: (layer, be[b], 0, 0)
    grid_spec = pltpu.PrefetchScalarGridSpec(
        num_scalar_prefetch=3,
        grid=(nb,),
        in_specs=[
            pl.BlockSpec((r, d), blk),
            pl.BlockSpec((1, 1, d, 2 * D_FF), wsel),
            pl.BlockSpec((1, 1, 1, 2 * D_FF), wsel),
            pl.BlockSpec((1, 1, D_FF, d), wsel),
            pl.BlockSpec((1, 1, 1, d), wsel),
        ],
        out_specs=pl.BlockSpec((r, d), lambda b, be, nv, nu: (b, 0)),
        scratch_shapes=[
            pltpu.VMEM((d, 2 * D_FF), BF16),
            pltpu.VMEM((D_FF, d), BF16),
            pltpu.SMEM((1,), jnp.int32),
        ],
    )
    return pl.pallas_call(
        _expert_kernel,
        grid_spec=grid_spec,
        out_shape=jax.ShapeDtypeStruct((rows, d), F32),
        compiler_params=_params("arbitrary"),
        name="moe_experts",
    )(block_e, nvalid, nused, xs, w_gu, b_gu, w_down, b_down)


def _combine_kernel(dest_ref, yb_ref, tg_ref, x_ref, mod_ref, nf_ref, o_ref, buf, sem, *, tc, final_norm):
    def issue(t, carry):
        for kk in range(TOP_K):
            pltpu.make_async_copy(yb_ref.at[pl.ds(dest_ref[0, 0, TOP_K * t + kk], 1)],
                                  buf.at[kk, pl.ds(t, 1)], sem).start(priority=kk % 2)
        return carry

    lax.fori_loop(0, tc, issue, 0, unroll=8)
    for kk in range(TOP_K):
        pltpu.make_async_copy(yb_ref.at[pl.ds(0, tc)], buf.at[kk], sem).wait()

    tg = tg_ref[...]
    y = tg[:, 0:1] * buf[0]
    for kk in range(1, TOP_K):
        y = y + tg[:, kk:kk + 1] * buf[kk]
    x = x_ref[...] + mod_ref[0][5:6] * y
    if final_norm:
        ms = jnp.mean(x * x, axis=-1, keepdims=True)
        x = x * lax.rsqrt(ms + NORM_EPS) * nf_ref[...]
    o_ref[...] = x


def moe_combine(yb, dest2, tg, x1, mod, norm_f, *, seq, tc, final_norm):
    n, d = x1.shape
    tpb = seq // tc
    row = lambda i: (i, 0)
    return pl.pallas_call(
        functools.partial(_combine_kernel, tc=tc, final_norm=final_norm),
        grid=(n // tc,),
        in_specs=[
            pl.BlockSpec((1, 1, TOP_K * tc), lambda i: (i, 0, 0), memory_space=pltpu.SMEM),
            pl.BlockSpec(memory_space=pl.ANY),
            pl.BlockSpec((tc, LANES), row),
            pl.BlockSpec((tc, d), row),
            pl.BlockSpec((1, 6, d), lambda i: (i // tpb, 0, 0)),
            pl.BlockSpec((1, d), lambda i: (0, 0)),
        ],
        out_specs=pl.BlockSpec((tc, d), row),
        out_shape=jax.ShapeDtypeStruct((n, d), F32),
        scratch_shapes=[pltpu.VMEM((TOP_K, tc, d), F32), pltpu.SemaphoreType.DMA],
        compiler_params=_params("arbitrary"),
        name="moe_combine",
    )(dest2, yb, tg, x1, mod, norm_f)


def _block_diag(blocks):
    g, m, _ = blocks.shape
    out = jnp.zeros((g * m, g * m), blocks.dtype)
    for i in range(g):
        out = out.at[i * m:(i + 1) * m, i * m:(i + 1) * m].set(blocks[i])
    return out


def _pad_cols(w, cols):
    return jnp.zeros(w.shape[:-1] + (cols,), w.dtype).at[..., :w.shape[-1]].set(w)


def trunk(x, c, ada_w, ada_b, norm1, w_in, w_out, rwkv_mu, rwkv_w0, rwkv_w_up, rwkv_a0, rwkv_a_up,
          rwkv_g_up, rwkv_k_k, rwkv_k_a, rwkv_r_k, rwkv_ln_w, rwkv_ln_b, pool_w, pool_scale, fox_b_f,
          norm2, router_w, router_b, moe_w_gu, moe_b_gu, moe_w_down, moe_b_down, norm_f,
          *, tm=512, tp=512, tq=512, expert_rows=256, td=512, tc=256, rwkv_passes=(1, 3, 1, 3)):
    batch, seq, d = x.shape
    depth = ada_w.shape[0]
    n = batch * seq
    mod_all = ada_modulation(c, ada_w, ada_b)
    x2 = x.reshape(n, d)
    nf = norm_f.reshape(1, d)
    b_gu4 = moe_b_gu.reshape(depth, N_EXPERTS, 1, 2 * D_FF)
    b_down4 = moe_b_down.reshape(depth, N_EXPERTS, 1, d)
    for l in range(depth):
        mod = mod_all[l]
        w_main = w_in[l][:, :MAIN_COLS].astype(BF16)
        w_f = _pad_cols(w_in[l][:, MAIN_COLS:], LANES)
        b_f = _pad_cols(fox_b_f[l].reshape(1, FOX_HEADS), LANES)
        w_lora = jnp.zeros((RWKV_LORA_IN, 3 * RWKV_DIM), F32)
        w_lora = w_lora.at[0:64, 0:RWKV_DIM].set(rwkv_w_up[l])
        w_lora = w_lora.at[64:128, RWKV_DIM:2 * RWKV_DIM].set(rwkv_a_up[l])
        w_lora = w_lora.at[128:256, 2 * RWKV_DIM:].set(rwkv_g_up[l]).astype(BF16)
        pvec = jnp.stack([rwkv_w0[l], rwkv_a0[l], rwkv_k_k[l], rwkv_k_a[l], rwkv_r_k[l].reshape(-1),
                          rwkv_ln_w[l], rwkv_ln_b[l], jnp.zeros((RWKV_DIM,), F32)], axis=0)
        w_pool = _block_diag(pool_w[l]).astype(BF16)
        rw = _pad_cols(router_w[l], LANES)
        rb = _pad_cols(router_b[l].reshape(1, N_EXPERTS), LANES)

        zr, zp, q, k, vt, cumc = in_projection(
            x2, mod, norm1[l].reshape(1, d), w_main, w_f, b_f, batch=batch, seq=seq, tm=tm, tv=tq)
        yr = rwkv_mixer(zr.reshape(batch, seq, RWKV_IN), rwkv_mu[l].reshape(1, RWKV_IN), pvec, w_lora,
                        passes=rwkv_passes)
        yp = pool_mixer(zp.reshape(batch, seq, POOL_DIM), w_pool, pool_scale[l].reshape(1, POOL_DIM), tp=tp)
        to3 = lambda t: t.reshape(batch, seq, t.shape[-1])
        yf = fox_attention(to3(q), to3(k), vt, to3(cumc), tq=tq)
        x1, h2, top_i, top_g = out_projection(
            yr.reshape(n, RWKV_DIM), yp.reshape(n, POOL_DIM), yf.reshape(n, FOX_DIM), x2, mod,
            w_out[l].astype(BF16), norm2[l].reshape(1, d), rw, rb, seq=seq, tm=tm)

        dest, block_e, nvalid, nused, pad, nb = _routing_tables(top_i, rows_per_block=expert_rows, tr=tm)
        xs = moe_dispatch(h2, dest.reshape(n // td, 1, TOP_K * td), pad, nused, nb=nb, r=expert_rows, td=td)
        yb = moe_experts(xs, block_e, nvalid, nused, moe_w_gu, b_gu4, moe_w_down, b_down4,
                         layer=l, r=expert_rows)
        x2 = moe_combine(yb, dest.reshape(n // tc, 1, TOP_K * tc), top_g, x1, mod, nf, seq=seq, tc=tc,
                         final_norm=(l == depth - 1))
    return x2.reshape(batch, seq, d)


def kernel(x, c, ada_w, ada_b, norm1, w_in, w_out, rwkv_mu, rwkv_w0, rwkv_w_up, rwkv_a0, rwkv_a_up, rwkv_g_up, rwkv_k_k, rwkv_k_a, rwkv_r_k, rwkv_ln_w, rwkv_ln_b, pool_w, pool_scale, fox_b_f, norm2, router_w, router_b, moe_w_gu, moe_b_gu, moe_w_down, moe_b_down, norm_f):
    return trunk(x, c, ada_w, ada_b, norm1, w_in, w_out, rwkv_mu, rwkv_w0, rwkv_w_up, rwkv_a0, rwkv_a_up,
                 rwkv_g_up, rwkv_k_k, rwkv_k_a, rwkv_r_k, rwkv_ln_w, rwkv_ln_b, pool_w, pool_scale, fox_b_f,
                 norm2, router_w, router_b, moe_w_gu, moe_b_gu, moe_w_down, moe_b_down, norm_f)
```

```python
import functools

import jax
import jax.numpy as jnp
from jax import lax
from jax.experimental import pallas as pl
from jax.experimental.pallas import tpu as pltpu

F32 = jnp.float32
BF16 = jnp.bfloat16

D_MODEL = 1024
HEAD_DIM = 64
RWKV_DIM = 256
RWKV_HEADS = 4
RWKV_LORA_IN = 256
RWKV_GN_EPS = 64e-5
POOL_DIM = 256
POOL_WINDOWS = (2, 4, 8, 16)
POOL_HALO = 16
FOX_DIM = 512
FOX_HEADS = 8
N_EXPERTS = 32
TOP_K = 4
D_FF = 1024
SWIGLU_LIMIT = 7.0
SWIGLU_ALPHA = 1.702
NORM_EPS = 1e-6
RWKV_IN = 1024
MAIN_COLS = RWKV_IN + POOL_DIM + 3 * FOX_DIM
LANES = 128
RWKV_CHUNK = 64
NEG_BIG = -1e30
VMEM_LIMIT = 48 * 1024 * 1024
NN = (((1,), (0,)), ((), ()))
NT = (((1,), (1,)), ((), ()))
BNN = (((2,), (1,)), ((0,), (0,)))
BNT = (((2,), (2,)), ((0,), (0,)))


def _dg(a, b, dims):
    return lax.dot_general(a, b, dims, preferred_element_type=F32)


def _split2(x):
    hi = x.astype(BF16)
    lo = (x - hi.astype(F32)).astype(BF16)
    return hi, lo


def _split3(x):
    hi = x.astype(BF16)
    r = x - hi.astype(F32)
    mid = r.astype(BF16)
    lo = (r - mid.astype(F32)).astype(BF16)
    return hi, mid, lo


def _mm(a, b, dims, passes=1):
    if passes == 1:
        return _dg(a.astype(BF16), b.astype(BF16), dims)
    ah, al = _split2(a)
    bh, bl = _split2(b)
    return _dg(ah, bh, dims) + (_dg(ah, bl, dims) + _dg(al, bh, dims))


def _mm_exact_lhs(a_bf16, b, dims):
    hi, mid, lo = _split3(b)
    return _dg(a_bf16, hi, dims) + (_dg(a_bf16, mid, dims) + _dg(a_bf16, lo, dims))


def _mm_exact_rhs(a, b_bf16, dims):
    hi, mid, lo = _split3(a)
    return _dg(hi, b_bf16, dims) + (_dg(mid, b_bf16, dims) + _dg(lo, b_bf16, dims))


def _sigmoid(x):
    return 1.0 / (1.0 + jnp.exp(-x))


def _softplus(x):
    return jnp.maximum(x, 0.0) + jnp.log1p(jnp.exp(-jnp.abs(x)))


def _params(*sem):
    return pltpu.CompilerParams(dimension_semantics=sem, vmem_limit_bytes=VMEM_LIMIT)


def _ada_kernel(c_ref, w_ref, b_ref, o_ref):
    c = c_ref[...]
    cond = c * _sigmoid(c)
    o_ref[0] = _mm(cond, w_ref[0], NN, passes=3) + b_ref[0]


def ada_modulation(c, ada_w, ada_b):
    L, D, D6 = ada_w.shape
    B = c.shape[0]
    rows = 8
    cp = jnp.zeros((rows, D), F32).at[:B].set(c)
    out = pl.pallas_call(
        _ada_kernel,
        grid=(L, D6 // D),
        in_specs=[
            pl.BlockSpec((rows, D), lambda l, j: (0, 0)),
            pl.BlockSpec((1, D, D), lambda l, j: (l, 0, j)),
            pl.BlockSpec((1, 1, D), lambda l, j: (l, 0, j)),
        ],
        out_specs=pl.BlockSpec((1, rows, D), lambda l, j: (l, 0, j)),
        out_shape=jax.ShapeDtypeStruct((L, rows, D6), F32),
        compiler_params=_params("arbitrary", "arbitrary"),
        name="ada_modulation",
    )(cp, ada_w, ada_b.reshape(L, 1, D6))
    return out[:, :B].reshape(L, B, D6 // D, D)


def _rms_modulate(x, gain, scale, shift):
    ms = jnp.mean(x * x, axis=-1, keepdims=True)
    return x * lax.rsqrt(ms + NORM_EPS) * gain * (1.0 + scale) + shift


def _inproj_kernel(x_ref, mod_ref, n1_ref, wm_ref, wf_ref, bf_ref,
                   zr_ref, zp_ref, q_ref, k_ref, vt_ref, cumc_ref, carry_ref,
                   *, tiles_per_batch, tm, tv):
    i = pl.program_id(0)

    @pl.when(i % tiles_per_batch == 0)
    def _():
        carry_ref[...] = jnp.zeros_like(carry_ref)

    mod = mod_ref[0]
    h = _rms_modulate(x_ref[...], n1_ref[...], mod[1:2], mod[0:1])
    z = _dg(h.astype(BF16), wm_ref[...], NN)
    zr_ref[...] = z[:, :RWKV_IN]
    zp_ref[...] = z[:, RWKV_IN:RWKV_IN + POOL_DIM]
    o = RWKV_IN + POOL_DIM
    q_ref[...] = (z[:, o:o + FOX_DIM] * (HEAD_DIM ** -0.5)).astype(BF16)
    k_ref[...] = z[:, o + FOX_DIM:o + 2 * FOX_DIM].astype(BF16)
    for u in range(tm // tv):
        vt_ref[0, u] = z[u * tv:(u + 1) * tv, o + 2 * FOX_DIM:o + 3 * FOX_DIM].T.astype(BF16)

    fl = _mm(h, wf_ref[...], NN, passes=3) + bf_ref[...]
    logf = jnp.minimum(fl, 0.0) - jnp.log1p(jnp.exp(-jnp.abs(fl)))
    row = lax.broadcasted_iota(jnp.int32, (tm, tm), 0)
    col = lax.broadcasted_iota(jnp.int32, (tm, tm), 1)
    tri = (row >= col).astype(BF16)
    cum = _mm_exact_lhs(tri, logf, NN) + carry_ref[...]
    carry_ref[...] = cum[tm - 1:tm, :]
    cumc_ref[...] = cum


def in_projection(x2, mod, norm1, w_main, w_f, b_f, *, batch, seq, tm, tv):
    n, d = x2.shape
    tpb = seq // tm
    row = lambda i: (i, 0)
    const = lambda i: (0, 0)
    out_shapes = (
        jax.ShapeDtypeStruct((n, RWKV_IN), F32),
        jax.ShapeDtypeStruct((n, POOL_DIM), F32),
        jax.ShapeDtypeStruct((n, FOX_DIM), BF16),
        jax.ShapeDtypeStruct((n, FOX_DIM), BF16),
        jax.ShapeDtypeStruct((batch, seq // tv, FOX_DIM, tv), BF16),
        jax.ShapeDtypeStruct((n, LANES), F32),
    )
    return pl.pallas_call(
        functools.partial(_inproj_kernel, tiles_per_batch=tpb, tm=tm, tv=tv),
        grid=(n // tm,),
        in_specs=[
            pl.BlockSpec((tm, d), row),
            pl.BlockSpec((1, 6, d), lambda i: (i // tpb, 0, 0)),
            pl.BlockSpec((1, d), const),
            pl.BlockSpec((d, MAIN_COLS), const),
            pl.BlockSpec((d, LANES), const),
            pl.BlockSpec((1, LANES), const),
        ],
        out_specs=(
            pl.BlockSpec((tm, RWKV_IN), row),
            pl.BlockSpec((tm, POOL_DIM), row),
            pl.BlockSpec((tm, FOX_DIM), row),
            pl.BlockSpec((tm, FOX_DIM), row),
            pl.BlockSpec((1, tm // tv, FOX_DIM, tv), lambda i: (i // tpb, i % tpb, 0, 0)),
            pl.BlockSpec((tm, LANES), row),
        ),
        out_shape=out_shapes,
        scratch_shapes=[pltpu.VMEM((1, LANES), F32)],
        compiler_params=_params("arbitrary"),
        name="in_projection",
    )(x2, mod, norm1, w_main, w_f, b_f)


def _head_blocks(y, same_head):
    return jnp.concatenate([y] * RWKV_HEADS, axis=1) * same_head


def _mm_shared(xs, y, dims, passes):
    sizes = [x.shape[1] for x in xs]
    x = xs[0] if len(xs) == 1 else jnp.concatenate(xs, axis=1)
    m = x.shape[1]
    if passes == 1:
        out = _dg(x.astype(BF16), y.astype(BF16), dims)
    else:
        xh, xl = _split2(x)
        yh, yl = y if isinstance(y, tuple) else _split2(y)
        both = _dg(jnp.concatenate([xh, xl], axis=1), yh, dims)
        out = both[:, :m] + (both[:, m:] + _dg(xh, yl, dims))
    outs, start = [], 0
    for s in sizes:
        outs.append(out[:, start:start + s])
        start += s
    return outs


def _mm_heads(xs, y, same_head, passes, dims):
    if passes == 1:
        return _mm_shared(xs, _head_blocks(y.astype(BF16), same_head), dims, 1)
    yh, yl = _split2(y)
    return _mm_shared(xs, (_head_blocks(yh, same_head), _head_blocks(yl, same_head)), dims, passes)


def _unit_lower_inverse(a, same_head, passes):
    c = a.shape[-2]
    ri = lax.broadcasted_iota(jnp.int32, a.shape[-2:], 0)
    ci = lax.broadcasted_iota(jnp.int32, a.shape[-2:], 1) % c
    same16 = (ri // 16) == (ci // 16)
    same32 = (ri // 32) == (ci // 32)
    eye = (ri == ci).astype(F32)
    mm = lambda us, v: _mm_heads(us, v, same_head, passes, BNN)
    a16 = jnp.where(same16, a, 0.0)
    t = eye + a16
    (p,) = mm([a16], a16)
    for _ in range(2):
        tp, p2 = mm([t, p], p)
        t = t + tp
        p = p2
    t = t + mm([t], p)[0]
    o16 = jnp.where(same32 & jnp.logical_not(same16), a, 0.0)
    t = t + mm([t], mm([o16], t)[0])[0]
    o32 = jnp.where(same32, 0.0, a)
    return t + mm([t], mm([o32], t)[0])[0]


def _rwkv_kernel(z_ref, mu_ref, pv_ref, wl_ref, y_ref, prev_ref, st_ref, *, nb, nc, passes):
    c = RWKV_CHUNK
    kd = RWKV_DIM
    g = nb * nc
    rows = nc * c
    p_score, p_inv, p_apply, p_state = passes

    @pl.when(pl.program_id(0) == 0)
    def _():
        prev_ref[...] = jnp.zeros_like(prev_ref)
        st_ref[...] = jnp.zeros_like(st_ref)

    mu = mu_ref[...]
    row0 = lax.broadcasted_iota(jnp.int32, (rows, RWKV_IN), 0) == 0
    zs = []
    for b in range(nb):
        zb = z_ref[b]
        prev = jnp.where(row0, prev_ref[b], pltpu.roll(zb, 1, 0))
        prev_ref[b] = zb[rows - 1:rows, :]
        zs.append(zb + (prev - zb) * mu)
    zf = jnp.stack(zs, axis=0).reshape(g * c, RWKV_IN)

    pv = pv_ref[...]
    w0, a0, k_k, k_a, r_k, ln_w, ln_b = (pv[i:i + 1] for i in range(7))
    r = zf[:, 0:kd]
    k = zf[:, kd:2 * kd]
    v = zf[:, 2 * kd:3 * kd]
    lo = zf[:, 3 * kd:]
    lane = lax.broadcasted_iota(jnp.int32, (1, kd), 1)
    lora_in = jnp.where(lane < 64, jnp.tanh(lo), jnp.where(lane < 128, lo, _sigmoid(lo)))
    lin = _dg(lora_in.astype(BF16), wl_ref[...], NN)
    lw = -jnp.exp(-_softplus(-(w0 + lin[:, 0:kd])) - 0.5)
    a = _sigmoid(a0 + lin[:, kd:2 * kd])
    gate = lin[:, 2 * kd:]

    hrow = lax.broadcasted_iota(jnp.int32, (kd, kd), 0) // HEAD_DIM
    hcol = lax.broadcasted_iota(jnp.int32, (kd, kd), 1) // HEAD_DIM
    same_head = hrow == hcol
    ones_bd = same_head.astype(BF16)
    head_sum = lambda t: _mm_exact_rhs(t, ones_bd, NN)

    kkr = k * k_k
    kk = kkr / jnp.maximum(jnp.sqrt(head_sum(kkr * kkr)), 1e-12)
    kp = k * (1.0 + (a - 1.0) * k_a)
    bonus = head_sum(r * kp * r_k) * v

    to3 = lambda t: t.reshape(g, c, kd)
    r3, kp3, v3, kk3, a3, lw3 = (to3(t) for t in (r, kp, v, kk, a, lw))

    lower = (lax.broadcasted_iota(jnp.int32, (c, c), 0) >= lax.broadcasted_iota(jnp.int32, (c, c), 1))
    tri = jnp.broadcast_to(lower.astype(BF16)[None], (g, c, c))
    cum = _mm_exact_lhs(tri, lw3, BNN)
    cum_end = cum[:, c - 1:c, :]
    e_in = jnp.exp(cum)
    e_out = jnp.exp(-cum)
    e_tail = jnp.exp(cum_end - cum)
    p_end = jnp.exp(cum_end)
    beta = kk3 * a3
    ab = -kk3 * jnp.exp(cum - lw3)
    rb = r3 * e_in
    bb = beta * e_out
    kb = kp3 * e_out
    bt = beta * e_tail
    kt = kp3 * e_tail

    ri = lax.broadcasted_iota(jnp.int32, (c, kd), 0)
    cj = lax.broadcasted_iota(jnp.int32, (c, kd), 1) % c
    strict = ri > cj
    incl = ri >= cj
    sab, srb = _mm_heads([ab, rb], bb, ones_bd, p_score, BNT)
    sak, srk = _mm_heads([ab, rb], kb, ones_bd, p_score, BNT)
    a_ab = jnp.where(strict, sab, 0.0)
    a_rb = jnp.where(incl, srb, 0.0)
    a_ak = jnp.where(strict, sak, 0.0)
    a_rk = jnp.where(incl, srk, 0.0)
    t_inv = _unit_lower_inverse(a_ab, ones_bd, p_inv)

    app = lambda us, w: _mm_heads(us, w, ones_bd, p_apply, BNN)
    akv, arkv = app([a_ak, a_rk], v3)
    (w_all,) = app([t_inv], ab)
    (u0_all,) = app([t_inv], akv)
    rq = rb + app([a_rb], w_all)[0]
    yc = app([a_rb], u0_all)[0] + arkv

    bt_t = jnp.swapaxes(bt, 1, 2)
    kt_t = jnp.swapaxes(kt, 1, 2)
    eye = (lax.broadcasted_iota(jnp.int32, (kd, kd), 0)
           == lax.broadcasted_iota(jnp.int32, (kd, kd), 1))
    full = lambda u, w: _mm(u, w, BNN, p_apply)
    trans = jnp.where(same_head, full(bt_t, w_all), 0.0) + jnp.where(eye, p_end, 0.0)
    inject = jnp.where(same_head, full(bt_t, u0_all) + full(kt_t, v3), 0.0)

    per_chunk = lambda t, j: t.reshape((nb, nc) + t.shape[1:])[:, j]
    st = st_ref[...]
    ys = []
    for j in range(nc):
        y_st, st_new = _mm_shared([per_chunk(rq, j), per_chunk(trans, j)], st, BNN, p_state)
        ys.append(y_st + per_chunk(yc, j))
        st = st_new + per_chunk(inject, j)
    st_ref[...] = st
    y = jnp.stack(ys, axis=1)

    y2 = y.reshape(g * c, kd)
    mean = head_sum(y2) * (1.0 / HEAD_DIM)
    dv = y2 - mean
    var = head_sum(dv * dv) * (1.0 / HEAD_DIM)
    out = (dv * lax.rsqrt(var + RWKV_GN_EPS) * ln_w + ln_b + bonus) * gate
    y_ref[...] = out.reshape(nb, rows, kd).astype(y_ref.dtype)


def rwkv_mixer(z3, mu, pvec, w_lora, *, passes=(3, 3, 3, 3), chunks_per_step=2):
    b, s, _ = z3.shape
    c = RWKV_CHUNK * chunks_per_step
    return pl.pallas_call(
        functools.partial(_rwkv_kernel, nb=b, nc=chunks_per_step, passes=passes),
        grid=(s // c,),
        in_specs=[
            pl.BlockSpec((b, c, RWKV_IN), lambda t: (0, t, 0)),
            pl.BlockSpec((1, RWKV_IN), lambda t: (0, 0)),
            pl.BlockSpec((8, RWKV_DIM), lambda t: (0, 0)),
            pl.BlockSpec((RWKV_LORA_IN, 3 * RWKV_DIM), lambda t: (0, 0)),
        ],
        out_specs=pl.BlockSpec((b, c, RWKV_DIM), lambda t: (0, t, 0)),
        out_shape=jax.ShapeDtypeStruct((b, s, RWKV_DIM), BF16),
        scratch_shapes=[
            pltpu.VMEM((b, 1, RWKV_IN), F32),
            pltpu.VMEM((b, RWKV_DIM, RWKV_DIM), F32),
        ],
        compiler_params=_params("arbitrary"),
        name="rwkv_mixer",
    )(z3, mu, pvec, w_lora)


def _pool_kernel(z_ref, w_ref, sc_ref, y_ref, ext_ref, *, tp):
    t = pl.program_id(1)
    hl = POOL_HALO
    levels = len(POOL_WINDOWS)

    @pl.when(t == 0)
    def _():
        ext_ref[:, 0:hl, :] = jnp.zeros((levels, hl, POOL_DIM), F32)

    z = z_ref[0]
    sums = []
    cur = z
    for lvl, w in enumerate(POOL_WINDOWS):
        ext_ref[lvl, hl:hl + tp, :] = cur
        back = w // 2
        cur = cur + ext_ref[lvl, hl - back:hl - back + tp, :]
        sums.append(cur)
    for lvl in range(levels):
        ext_ref[lvl, 0:hl, :] = ext_ref[lvl, tp:tp + hl, :]

    lane = lax.broadcasted_iota(jnp.int32, (tp, POOL_DIM), 1)
    pos = (t * tp + lax.broadcasted_iota(jnp.int32, (tp, POOL_DIM), 0) + 1).astype(F32)
    grp = POOL_DIM // levels
    win_sum = sums[-1]
    count = jnp.minimum(pos, float(POOL_WINDOWS[-1]))
    for lvl in range(levels - 2, -1, -1):
        in_grp = lane < (lvl + 1) * grp
        win_sum = jnp.where(in_grp, sums[lvl], win_sum)
        count = jnp.where(in_grp, jnp.minimum(pos, float(POOL_WINDOWS[lvl])), count)
    pooled = win_sum / count - z
    y = _dg(pooled.astype(BF16), w_ref[...], NN) * sc_ref[...]
    y_ref[0] = y.astype(y_ref.dtype)


def pool_mixer(zp3, w_bd, scale, *, tp):
    b, s, _ = zp3.shape
    return pl.pallas_call(
        functools.partial(_pool_kernel, tp=tp),
        grid=(b, s // tp),
        in_specs=[
            pl.BlockSpec((1, tp, POOL_DIM), lambda i, t: (i, t, 0)),
            pl.BlockSpec((POOL_DIM, POOL_DIM), lambda i, t: (0, 0)),
            pl.BlockSpec((1, POOL_DIM), lambda i, t: (0, 0)),
        ],
        out_specs=pl.BlockSpec((1, tp, POOL_DIM), lambda i, t: (i, t, 0)),
        out_shape=jax.ShapeDtypeStruct((b, s, POOL_DIM), BF16),
        scratch_shapes=[pltpu.VMEM((len(POOL_WINDOWS), POOL_HALO + tp, POOL_DIM), F32)],
        compiler_params=_params("arbitrary", "arbitrary"),
        name="pool_mixer",
    )(zp3, w_bd, scale)


def _fox_kernel(q_ref, k_ref, vt_ref, cc_ref, o_ref, ckb_ref, m_ref, l_ref, acc_ref, sa_ref, sb_ref,
                *, tq, nblk):
    hp = pl.program_id(1)
    i = pl.program_id(2)
    lane = lax.broadcasted_iota(jnp.int32, (1, LANES), 1)
    reps = tq // LANES

    @pl.when(i == 0)
    def _():
        def fill(jb, carry):
            start = pl.multiple_of(jb * tq, tq)
            blk = cc_ref[0, pl.ds(start, tq), :]
            for hh in range(2):
                col = jnp.sum(jnp.where(lane == 2 * hp + hh, blk, 0.0), axis=-1, keepdims=True)
                ckb_ref[hh, pl.ds(start, tq), :] = jnp.broadcast_to(col, (tq, LANES))
            return carry

        lax.fori_loop(0, nblk, fill, 0)

    q2 = q_ref[0]
    m_ref[...] = jnp.full(m_ref.shape, NEG_BIG, F32)
    l_ref[...] = jnp.zeros(l_ref.shape, F32)
    acc_ref[...] = jnp.zeros(acc_ref.shape, F32)
    qh = [jnp.where(lane // HEAD_DIM == hh, q2, jnp.zeros_like(q2)) for hh in range(2)]

    def scores_into(j, dst):
        k2 = k_ref[0, pl.ds(pl.multiple_of(j * tq, tq), tq), :]
        for hh in range(2):
            dst[hh] = _dg(k2, qh[hh], NT)

    def consume(j, src, masked):
        start = pl.multiple_of(j * tq, tq)
        for hh in range(2):
            ck = ckb_ref[hh, pl.ds(start, tq), :]
            s = src[hh] - jnp.concatenate([ck] * reps, axis=1)
            if masked:
                kpos = lax.broadcasted_iota(jnp.int32, (tq, tq), 0)
                qpos = lax.broadcasted_iota(jnp.int32, (tq, tq), 1)
                s = jnp.where(kpos <= qpos, s, NEG_BIG)
            m_old = m_ref[hh][0:1, :]
            m_new = jnp.maximum(m_old, jnp.max(s, axis=0, keepdims=True))
            alpha = jnp.exp(m_old - m_new)
            p = jnp.exp(s - m_new)
            l_new = alpha * l_ref[hh][0:1, :] + jnp.sum(p, axis=0, keepdims=True)
            vth = vt_ref[0, j, hh * HEAD_DIM:(hh + 1) * HEAD_DIM, :]
            acc_ref[hh] = alpha * acc_ref[hh] + _dg(vth, p.astype(BF16), NN)
            l_ref[hh] = jnp.broadcast_to(l_new, (8, tq))
            m_ref[hh] = jnp.broadcast_to(m_new, (8, tq))

    scores_into(0, sa_ref)

    def pair(jj, carry):
        j = 2 * jj
        scores_into(j + 1, sb_ref)
        consume(j, sa_ref, False)
        scores_into(j + 2, sa_ref)
        consume(j + 1, sb_ref, False)
        return carry

    lax.fori_loop(0, i // 2, pair, 0)

    @pl.when(i % 2 == 0)
    def _():
        consume(i, sa_ref, True)

    @pl.when(i % 2 == 1)
    def _():
        scores_into(i, sb_ref)
        consume(i - 1, sa_ref, False)
        consume(i, sb_ref, True)

    ot = jnp.concatenate([acc_ref[hh] / l_ref[hh][0:1, :] for hh in range(2)], axis=0)
    o_ref[0] = ot.T.astype(o_ref.dtype)


def fox_attention(q3, k3, vt4, cumc3, *, tq):
    b, s, _ = q3.shape
    nq = s // tq
    pairs = FOX_HEADS // 2
    return pl.pallas_call(
        functools.partial(_fox_kernel, tq=tq, nblk=nq),
        grid=(b, pairs, nq),
        in_specs=[
            pl.BlockSpec((1, tq, LANES), lambda bi, hp, i: (bi, i, hp)),
            pl.BlockSpec((1, s, LANES), lambda bi, hp, i: (bi, 0, hp)),
            pl.BlockSpec((1, nq, LANES, tq), lambda bi, hp, i: (bi, 0, hp, 0)),
            pl.BlockSpec((1, s, LANES), lambda bi, hp, i: (bi, 0, 0)),
        ],
        out_specs=pl.BlockSpec((1, tq, LANES), lambda bi, hp, i: (bi, i, hp)),
        out_shape=jax.ShapeDtypeStruct((b, s, FOX_DIM), BF16),
        scratch_shapes=[
            pltpu.VMEM((2, s, LANES), F32),
            pltpu.VMEM((2, 8, tq), F32),
            pltpu.VMEM((2, 8, tq), F32),
            pltpu.VMEM((2, HEAD_DIM, tq), F32),
            pltpu.VMEM((2, tq, tq), F32),
            pltpu.VMEM((2, tq, tq), F32),
        ],
        compiler_params=_params("arbitrary", "arbitrary", "arbitrary"),
        name="fox_attention",
    )(q3, k3, vt4, cumc3)


def _outproj_kernel(yr_ref, yp_ref, yf_ref, x_ref, mod_ref, wo_ref, n2_ref, rw_ref, rb_ref,
                    x1_ref, h2_ref, ti_ref, tg_ref):
    mod = mod_ref[0]
    mixed = (_dg(yr_ref[...], wo_ref[0:RWKV_DIM, :], NN)
             + _dg(yp_ref[...], wo_ref[RWKV_DIM:RWKV_DIM + POOL_DIM, :], NN)
             + _dg(yf_ref[...], wo_ref[RWKV_DIM + POOL_DIM:, :], NN))
    x1 = x_ref[...] + mod[2:3] * mixed
    x1_ref[...] = x1
    h2 = _rms_modulate(x1, n2_ref[...], mod[4:5], mod[3:4])
    h2_ref[...] = h2

    tm = x1.shape[0]
    lane = lax.broadcasted_iota(jnp.int32, (tm, LANES), 1)
    lane_f = lane.astype(F32)
    logits = _mm(h2, rw_ref[...], NN, passes=3) + rb_ref[...]
    work = jnp.where(lane < N_EXPERTS, logits, NEG_BIG)
    top_i = jnp.zeros((tm, LANES), F32)
    top_e = jnp.zeros((tm, LANES), F32)
    v0 = None
    for kk in range(TOP_K):
        vmax = jnp.max(work, axis=-1, keepdims=True)
        idx = jnp.min(jnp.where(work == vmax, lane_f, float(LANES)), axis=-1, keepdims=True)
        if kk == 0:
            v0 = vmax
        top_i = jnp.where(lane == kk, idx, top_i)
        top_e = jnp.where(lane == kk, jnp.exp(vmax - v0), top_e)
        work = jnp.where(lane_f == idx, NEG_BIG, work)
    ti_ref[...] = top_i.astype(jnp.int32)
    tg_ref[...] = top_e / jnp.sum(top_e, axis=-1, keepdims=True)


def out_projection(yr, yp, yf, x2, mod, w_out, norm2, rw, rb, *, seq, tm):
    n, d = x2.shape
    tpb = seq // tm
    row = lambda i: (i, 0)
    const = lambda i: (0, 0)
    return pl.pallas_call(
        _outproj_kernel,
        grid=(n // tm,),
        in_specs=[
            pl.BlockSpec((tm, RWKV_DIM), row),
            pl.BlockSpec((tm, POOL_DIM), row),
            pl.BlockSpec((tm, FOX_DIM), row),
            pl.BlockSpec((tm, d), row),
            pl.BlockSpec((1, 6, d), lambda i: (i // tpb, 0, 0)),
            pl.BlockSpec((d, d), const),
            pl.BlockSpec((1, d), const),
            pl.BlockSpec((d, LANES), const),
            pl.BlockSpec((1, LANES), const),
        ],
        out_specs=(
            pl.BlockSpec((tm, d), row),
            pl.BlockSpec((tm, d), row),
            pl.BlockSpec((tm, LANES), row),
            pl.BlockSpec((tm, LANES), row),
        ),
        out_shape=(
            jax.ShapeDtypeStruct((n, d), F32),
            jax.ShapeDtypeStruct((n, d), F32),
            jax.ShapeDtypeStruct((n, LANES), jnp.int32),
            jax.ShapeDtypeStruct((n, LANES), F32),
        ),
        compiler_params=_params("arbitrary"),
        name="out_projection",
    )(yr, yp, yf, x2, mod, w_out, norm2, rw, rb)


def _rank_kernel(ti_ref, rank_ref, cnt_ref, carry_ref, *, tr):
    @pl.when(pl.program_id(0) == 0)
    def _():
        carry_ref[...] = jnp.zeros_like(carry_ref)

    ti = ti_ref[...]
    lane = lax.broadcasted_iota(jnp.int32, (tr, LANES), 1)
    hits = [ti[:, kk:kk + 1] == lane for kk in range(TOP_K)]
    cnt = hits[0].astype(F32)
    for kk in range(1, TOP_K):
        cnt = cnt + hits[kk].astype(F32)
    before = (lax.broadcasted_iota(jnp.int32, (tr, tr), 0)
              > lax.broadcasted_iota(jnp.int32, (tr, tr), 1)).astype(BF16)
    pre = _dg(before, cnt.astype(BF16), NN) + carry_ref[...]
    rank = jnp.zeros((tr, LANES), F32)
    for kk in range(TOP_K):
        rk = jnp.sum(jnp.where(hits[kk], pre, 0.0), axis=-1, keepdims=True)
        rank = jnp.where(lane == kk, rk, rank)
    rank_ref[...] = rank.astype(jnp.int32)
    total = carry_ref[...] + jnp.sum(cnt, axis=0, keepdims=True)
    carry_ref[...] = total
    cnt_ref[...] = jnp.broadcast_to(total, cnt_ref.shape)


def moe_rank(top_i, *, tr):
    n = top_i.shape[0]
    return pl.pallas_call(
        functools.partial(_rank_kernel, tr=tr),
        grid=(n // tr,),
        in_specs=[pl.BlockSpec((tr, LANES), lambda i: (i, 0))],
        out_specs=(pl.BlockSpec((tr, LANES), lambda i: (i, 0)),
                   pl.BlockSpec((8, LANES), lambda i: (0, 0))),
        out_shape=(jax.ShapeDtypeStruct((n, LANES), jnp.int32),
                   jax.ShapeDtypeStruct((8, LANES), F32)),
        scratch_shapes=[pltpu.VMEM((1, LANES), F32)],
        compiler_params=_params("arbitrary"),
        name="moe_rank",
    )(top_i)


def _routing_tables(top_i, *, rows_per_block, tr):
    n = top_i.shape[0]
    r = rows_per_block
    rank, cnt = moe_rank(top_i, tr=tr)
    counts = cnt[0, :N_EXPERTS].astype(jnp.int32)
    padded = (counts + r - 1) // r * r
    pend = jnp.cumsum(padded)
    pstart = pend - padded
    e_sel = top_i[:, :TOP_K]
    onehot = e_sel[:, :, None] == jnp.arange(N_EXPERTS, dtype=jnp.int32)[None, None, :]
    dest = (jnp.sum(jnp.where(onehot, pstart[None, None, :], 0), axis=-1) + rank[:, :TOP_K]).astype(jnp.int32)
    nb = -(-(n * TOP_K + N_EXPERTS * (r - 1)) // r)
    bstart = jnp.arange(nb, dtype=jnp.int32) * r
    block_e = jnp.minimum(jnp.sum(bstart[:, None] >= pend[None, :], axis=1), N_EXPERTS - 1).astype(jnp.int32)
    nk = n * TOP_K
    rows = nb * r
    kk = jnp.arange(TOP_K, dtype=jnp.int32)[None, :]
    y4_row = (kk * n + jnp.arange(n, dtype=jnp.int32)[:, None]).reshape(nk)
    spare = nk + jnp.arange(rows + r, dtype=jnp.int32) % (2 * r)
    slot = spare.at[dest.reshape(nk)].set(y4_row, unique_indices=True, indices_are_sorted=False)
    slot = slot.at[rows:].set(nk + r + jnp.arange(r, dtype=jnp.int32))
    tok = jnp.where(slot[:rows] < nk, slot[:rows] % n, 0)
    return block_e, tok.reshape(nb, 1, r), slot.reshape(nb + 1, 1, r)


def _zero_fill(pad_ref, nu_ref, xs_ref, zero_ref, zsem, *, r, nb):
    row_sem, block_sem = zsem.at[0], zsem.at[1]
    zero_ref[...] = jnp.zeros_like(zero_ref)

    def each_copy(fn):
        def per_expert(e, carry):
            start = pad_ref[0, e]

            def per_row(i, c2):
                fn(pltpu.make_async_copy(zero_ref.at[pl.ds(0, 1)], xs_ref.at[pl.ds(start + i, 1)], row_sem))
                return c2

            lax.fori_loop(0, pad_ref[1, e], per_row, 0)
            return carry

        lax.fori_loop(0, N_EXPERTS, per_expert, 0)

        def per_block(b, carry):
            fn(pltpu.make_async_copy(zero_ref, xs_ref.at[pl.ds(pl.multiple_of(b * r, r), r)], block_sem))
            return carry

        lax.fori_loop(nu_ref[0], nb, per_block, 0)

    each_copy(lambda cp: cp.start())
    each_copy(lambda cp: cp.wait())


def _dispatch_kernel(dest_ref, pad_ref, nu_ref, h_ref, xs_ref, zero_ref, sem, zsem, *, td, r, nb):
    @pl.when(pl.program_id(0) == 0)
    def _():
        _zero_fill(pad_ref, nu_ref, xs_ref, zero_ref, zsem, r=r, nb=nb)

    def issue(t, carry):
        for kk in range(TOP_K):
            pltpu.make_async_copy(h_ref.at[pl.ds(t, 1)],
                                  xs_ref.at[pl.ds(dest_ref[0, 0, TOP_K * t + kk], 1)], sem
                                  ).start(priority=kk % 2)
        return carry

    lax.fori_loop(0, td, issue, 0, unroll=8)
    for kk in range(TOP_K):
        pltpu.make_async_copy(h_ref, xs_ref.at[pl.ds(0, td)], sem).wait()


def moe_dispatch(h2, dest2, pad, nused, *, nb, r, td):
    n, d = h2.shape
    return pl.pallas_call(
        functools.partial(_dispatch_kernel, td=td, r=r, nb=nb),
        grid=(n // td,),
        in_specs=[
            pl.BlockSpec((1, 1, TOP_K * td), lambda i: (i, 0, 0), memory_space=pltpu.SMEM),
            pl.BlockSpec(memory_space=pltpu.SMEM),
            pl.BlockSpec(memory_space=pltpu.SMEM),
            pl.BlockSpec((td, d), lambda i: (i, 0)),
        ],
        out_specs=pl.BlockSpec(memory_space=pl.ANY),
        out_shape=jax.ShapeDtypeStruct((nb * r, d), F32),
        scratch_shapes=[pltpu.VMEM((r, d), F32), pltpu.SemaphoreType.DMA, pltpu.SemaphoreType.DMA((2,))],
        compiler_params=_params("arbitrary"),
        name="moe_dispatch",
    )(dest2, pad, nused, h2)


def _expert_kernel(be_ref, nv_ref, nu_ref, x_ref, wgu_ref, bgu_ref, wd_ref, bd_ref, y_ref,
                   wgu_bf, wd_bf, cur_ref):
    b = pl.program_id(0)
    e = be_ref[b]
    nv = nv_ref[b]

    @pl.when(b == 0)
    def _():
        cur_ref[0] = -1

    @pl.when((nv > 0) & (cur_ref[0] != e))
    def _():
        cur_ref[0] = e
        step = 128

        def cast(i, carry):
            s = pl.multiple_of(i * step, step)
            wgu_bf[pl.ds(s, step), :] = wgu_ref[0, 0, pl.ds(s, step), :].astype(BF16)
            wd_bf[pl.ds(s, step), :] = wd_ref[0, 0, pl.ds(s, step), :].astype(BF16)
            return carry

        lax.fori_loop(0, D_MODEL // step, cast, 0)

    @pl.when(nv == 0)
    def _():
        y_ref[...] = jnp.zeros_like(y_ref)

    @pl.when(nv > 0)
    def _():
        x = x_ref[...].astype(BF16)
        gu = _dg(x, wgu_bf[...], NN) + bgu_ref[0, 0]
        gate = jnp.minimum(gu[:, :D_FF], SWIGLU_LIMIT)
        up = jnp.clip(gu[:, D_FF:], -SWIGLU_LIMIT, SWIGLU_LIMIT)
        act = (up + 1.0) * (gate * _sigmoid(SWIGLU_ALPHA * gate))
        y_ref[...] = _dg(act.astype(BF16), wd_bf[...], NN) + bd_ref[0, 0]


def moe_experts(xs, block_e, nvalid, nused, w_gu, b_gu, w_down, b_down, *, layer, r):
    rows, d = xs.shape
    nb = rows // r
    blk = lambda b, be, nv, nu: (jnp.minimum(b, nu[0] - 1), 0)
    wsel = lambda b, be, nv, nu: (layer, be[b], 0, 0)
    grid_spec = pltpu.PrefetchScalarGridSpec(
        num_scalar_prefetch=3,
        grid=(nb,),
        in_specs=[
            pl.BlockSpec((r, d), blk),
            pl.BlockSpec((1, 1, d, 2 * D_FF), wsel),
            pl.BlockSpec((1, 1, 1, 2 * D_FF), wsel),
            pl.BlockSpec((1, 1, D_FF, d), wsel),
            pl.BlockSpec((1, 1, 1, d), wsel),
        ],
        out_specs=pl.BlockSpec((r, d), lambda b, be, nv, nu: (b, 0)),
        scratch_shapes=[
            pltpu.VMEM((d, 2 * D_FF), BF16),
            pltpu.VMEM((D_FF, d), BF16),
            pltpu.SMEM((1,), jnp.int32),
        ],
    )
    return pl.pallas_call(
        _expert_kernel,
        grid_spec=grid_spec,
        out_shape=jax.ShapeDtypeStruct((rows, d), F32),
        compiler_params=_params("arbitrary"),
        name="moe_experts",
    )(block_e, nvalid, nused, xs, w_gu, b_gu, w_down, b_down)


def _combine_kernel(dest_ref, yb_ref, tg_ref, x_ref, mod_ref, nf_ref, o_ref, buf, sem, *, tc, final_norm):
    def issue(t, carry):
        for kk in range(TOP_K):
            pltpu.make_async_copy(yb_ref.at[pl.ds(dest_ref[0, 0, TOP_K * t + kk], 1)],
                                  buf.at[kk, pl.ds(t, 1)], sem).start(priority=kk % 2)
        return carry

    lax.fori_loop(0, tc, issue, 0, unroll=8)
    for kk in range(TOP_K):
        pltpu.make_async_copy(yb_ref.at[pl.ds(0, tc)], buf.at[kk], sem).wait()

    tg = tg_ref[...]
    y = tg[:, 0:1] * buf[0]
    for kk in range(1, TOP_K):
        y = y + tg[:, kk:kk + 1] * buf[kk]
    x = x_ref[...] + mod_ref[0][5:6] * y
    if final_norm:
        ms = jnp.mean(x * x, axis=-1, keepdims=True)
        x = x * lax.rsqrt(ms + NORM_EPS) * nf_ref[...]
    o_ref[...] = x


def moe_combine(yb, dest2, tg, x1, mod, norm_f, *, seq, tc, final_norm):
    n, d = x1.shape
    tpb = seq // tc
    row = lambda i: (i, 0)
    return pl.pallas_call(
        functools.partial(_combine_kernel, tc=tc, final_norm=final_norm),
        grid=(n // tc,),
        in_specs=[
            pl.BlockSpec((1, 1, TOP_K * tc), lambda i: (i, 0, 0), memory_space=pltpu.SMEM),
            pl.BlockSpec(memory_space=pl.ANY),
            pl.BlockSpec((tc, LANES), row),
            pl.BlockSpec((tc, d), row),
            pl.BlockSpec((1, 6, d), lambda i: (i // tpb, 0, 0)),
            pl.BlockSpec((1, d), lambda i: (0, 0)),
        ],
        out_specs=pl.BlockSpec((tc, d), row),
        out_shape=jax.ShapeDtypeStruct((n, d), F32),
        scratch_shapes=[pltpu.VMEM((TOP_K, tc, d), F32), pltpu.SemaphoreType.DMA],
        compiler_params=_params("arbitrary"),
        name="moe_combine",
    )(dest2, yb, tg, x1, mod, norm_f)


def _moe_kernel(be_ref, tok0_ref, tokn_ref, slotp_ref, slotc_ref, h_ref, wgu_ref, bgu_ref, wd_ref, bd_ref,
                y4_ref, xbuf, ybuf, wgu_bf, wd_bf, cur_ref, gsem, ssem, *, r, nb, nk):
    b = pl.program_id(0)
    cur = b % 2
    nxt = 1 - cur

    def gather(tab_ref, half):
        for t in range(r):
            pltpu.make_async_copy(h_ref.at[pl.ds(tab_ref[0, 0, t], 1)], xbuf.at[half, pl.ds(t, 1)],
                                  gsem.at[half]).start(priority=t % 2)

    def scatter(tab_ref, half):
        for t in range(r):
            pltpu.make_async_copy(ybuf.at[half, pl.ds(t, 1)], y4_ref.at[pl.ds(tab_ref[0, 0, t], 1)],
                                  ssem.at[half]).start(priority=t % 2)

    gather_wait = lambda half: pltpu.make_async_copy(h_ref.at[pl.ds(0, r)], xbuf.at[half], gsem.at[half]).wait()
    scatter_wait = lambda half: pltpu.make_async_copy(ybuf.at[half], y4_ref.at[pl.ds(0, r)], ssem.at[half]).wait()

    @pl.when(b == 0)
    def _():
        cur_ref[0] = -1
        ybuf[...] = jnp.zeros_like(ybuf)
        for half in range(2):
            spare = pltpu.make_async_copy(ybuf.at[half], y4_ref.at[pl.ds(nk + half * r, r)], ssem.at[half])
            spare.start()
            spare.wait()
        gather(tok0_ref, 0)

    gather_wait(cur)

    @pl.when(b >= 1)
    def _():
        scatter_wait(cur)

    e = be_ref[b]

    @pl.when(cur_ref[0] != e)
    def _():
        cur_ref[0] = e
        step = 128

        def cast(i, carry):
            s = pl.multiple_of(i * step, step)
            wgu_bf[pl.ds(s, step), :] = wgu_ref[0, 0, pl.ds(s, step), :].astype(BF16)
            wd_bf[pl.ds(s, step), :] = wd_ref[0, 0, pl.ds(s, step), :].astype(BF16)
            return carry

        lax.fori_loop(0, D_MODEL // step, cast, 0)

    x = xbuf[cur].astype(BF16)
    gather(tokn_ref, nxt)
    scatter(slotp_ref, nxt)
    gu = _dg(x, wgu_bf[...], NN) + bgu_ref[0, 0]
    gate = jnp.minimum(gu[:, :D_FF], SWIGLU_LIMIT)
    up = jnp.clip(gu[:, D_FF:], -SWIGLU_LIMIT, SWIGLU_LIMIT)
    act = (up + 1.0) * (gate * _sigmoid(SWIGLU_ALPHA * gate))
    ybuf[cur] = _dg(act.astype(BF16), wd_bf[...], NN) + bd_ref[0, 0]

    @pl.when(b == nb - 1)
    def _():
        scatter(slotc_ref, cur)
        gather_wait(nxt)
        scatter_wait(nxt)
        scatter_wait(cur)


def moe_experts_fused(h2, block_e, tok_tab, slot_tab, w_gu, b_gu, w_down, b_down, *, layer, r):
    n, d = h2.shape
    nb = tok_tab.shape[0]
    nk = TOP_K * n
    smem = lambda imap: pl.BlockSpec((1, 1, r), imap, memory_space=pltpu.SMEM)
    wsel = lambda b, be: (layer, be[b], 0, 0)
    grid_spec = pltpu.PrefetchScalarGridSpec(
        num_scalar_prefetch=1,
        grid=(nb,),
        in_specs=[
            smem(lambda b, be: (0, 0, 0)),
            smem(lambda b, be: (jnp.minimum(b + 1, nb - 1), 0, 0)),
            smem(lambda b, be: (jnp.where(b == 0, nb, b - 1), 0, 0)),
            smem(lambda b, be: (b, 0, 0)),
            pl.BlockSpec(memory_space=pl.ANY),
            pl.BlockSpec((1, 1, d, 2 * D_FF), wsel),
            pl.BlockSpec((1, 1, 1, 2 * D_FF), wsel),
            pl.BlockSpec((1, 1, D_FF, d), wsel),
            pl.BlockSpec((1, 1, 1, d), wsel),
        ],
        out_specs=pl.BlockSpec(memory_space=pl.ANY),
        scratch_shapes=[
            pltpu.VMEM((2, r, d), F32),
            pltpu.VMEM((2, r, d), F32),
            pltpu.VMEM((d, 2 * D_FF), BF16),
            pltpu.VMEM((D_FF, d), BF16),
            pltpu.SMEM((1,), jnp.int32),
            pltpu.SemaphoreType.DMA((2,)),
            pltpu.SemaphoreType.DMA((2,)),
        ],
    )
    return pl.pallas_call(
        functools.partial(_moe_kernel, r=r, nb=nb, nk=nk),
        grid_spec=grid_spec,
        out_shape=jax.ShapeDtypeStruct((nk + 2 * r, d), F32),
        compiler_params=_params("arbitrary"),
        name="moe_experts",
    )(block_e, tok_tab, tok_tab, slot_tab, slot_tab, h2, w_gu, b_gu, w_down, b_down)


def _sum_kernel(y0_ref, y1_ref, y2_ref, y3_ref, tg_ref, x_ref, mod_ref, nf_ref, o_ref, *, final_norm):
    tg = tg_ref[...]
    y = tg[:, 0:1] * y0_ref[...]
    for kk, y_ref in enumerate((y1_ref, y2_ref, y3_ref), start=1):
        y = y + tg[:, kk:kk + 1] * y_ref[...]
    x = x_ref[...] + mod_ref[0][5:6] * y
    if final_norm:
        ms = jnp.mean(x * x, axis=-1, keepdims=True)
        x = x * lax.rsqrt(ms + NORM_EPS) * nf_ref[...]
    o_ref[...] = x


def moe_sum(y4, tg, x1, mod, norm_f, *, seq, tc, final_norm):
    n, d = x1.shape
    tpb = seq // tc
    nt = n // tc
    row = lambda i: (i, 0)
    plane = lambda kk: pl.BlockSpec((tc, d), lambda i: (kk * nt + i, 0))
    return pl.pallas_call(
        functools.partial(_sum_kernel, final_norm=final_norm),
        grid=(nt,),
        in_specs=[plane(0), plane(1), plane(2), plane(3),
                  pl.BlockSpec((tc, LANES), row),
                  pl.BlockSpec((tc, d), row),
                  pl.BlockSpec((1, 6, d), lambda i: (i // tpb, 0, 0)),
                  pl.BlockSpec((1, d), lambda i: (0, 0))],
        out_specs=pl.BlockSpec((tc, d), row),
        out_shape=jax.ShapeDtypeStruct((n, d), F32),
        compiler_params=_params("arbitrary"),
        name="moe_sum",
    )(y4, y4, y4, y4, tg, x1, mod, norm_f)


def _block_diag(blocks):
    g, m, _ = blocks.shape
    out = jnp.zeros((g * m, g * m), blocks.dtype)
    for i in range(g):
        out = out.at[i * m:(i + 1) * m, i * m:(i + 1) * m].set(blocks[i])
    return out


def _pad_cols(w, cols):
    return jnp.zeros(w.shape[:-1] + (cols,), w.dtype).at[..., :w.shape[-1]].set(w)


def trunk(x, c, ada_w, ada_b, norm1, w_in, w_out, rwkv_mu, rwkv_w0, rwkv_w_up, rwkv_a0, rwkv_a_up,
          rwkv_g_up, rwkv_k_k, rwkv_k_a, rwkv_r_k, rwkv_ln_w, rwkv_ln_b, pool_w, pool_scale, fox_b_f,
          norm2, router_w, router_b, moe_w_gu, moe_b_gu, moe_w_down, moe_b_down, norm_f,
          *, tm=512, tp=512, tq=512, expert_rows=256, tc=512, rwkv_passes=(1, 3, 1, 3)):
    batch, seq, d = x.shape
    depth = ada_w.shape[0]
    n = batch * seq
    mod_all = ada_modulation(c, ada_w, ada_b)
    x2 = x.reshape(n, d)
    nf = norm_f.reshape(1, d)
    b_gu4 = moe_b_gu.reshape(depth, N_EXPERTS, 1, 2 * D_FF)
    b_down4 = moe_b_down.reshape(depth, N_EXPERTS, 1, d)
    for l in range(depth):
        mod = mod_all[l]
        w_main = w_in[l][:, :MAIN_COLS].astype(BF16)
        w_f = _pad_cols(w_in[l][:, MAIN_COLS:], LANES)
        b_f = _pad_cols(fox_b_f[l].reshape(1, FOX_HEADS), LANES)
        w_lora = jnp.zeros((RWKV_LORA_IN, 3 * RWKV_DIM), F32)
        w_lora = w_lora.at[0:64, 0:RWKV_DIM].set(rwkv_w_up[l])
        w_lora = w_lora.at[64:128, RWKV_DIM:2 * RWKV_DIM].set(rwkv_a_up[l])
        w_lora = w_lora.at[128:256, 2 * RWKV_DIM:].set(rwkv_g_up[l]).astype(BF16)
        pvec = jnp.stack([rwkv_w0[l], rwkv_a0[l], rwkv_k_k[l], rwkv_k_a[l], rwkv_r_k[l].reshape(-1),
                          rwkv_ln_w[l], rwkv_ln_b[l], jnp.zeros((RWKV_DIM,), F32)], axis=0)
        w_pool = _block_diag(pool_w[l]).astype(BF16)
        rw = _pad_cols(router_w[l], LANES)
        rb = _pad_cols(router_b[l].reshape(1, N_EXPERTS), LANES)

        zr, zp, q, k, vt, cumc = in_projection(
            x2, mod, norm1[l].reshape(1, d), w_main, w_f, b_f, batch=batch, seq=seq, tm=tm, tv=tq)
        yr = rwkv_mixer(zr.reshape(batch, seq, RWKV_IN), rwkv_mu[l].reshape(1, RWKV_IN), pvec, w_lora,
                        passes=rwkv_passes)
        yp = pool_mixer(zp.reshape(batch, seq, POOL_DIM), w_pool, pool_scale[l].reshape(1, POOL_DIM), tp=tp)
        to3 = lambda t: t.reshape(batch, seq, t.shape[-1])
        yf = fox_attention(to3(q), to3(k), vt, to3(cumc), tq=tq)
        x1, h2, top_i, top_g = out_projection(
            yr.reshape(n, RWKV_DIM), yp.reshape(n, POOL_DIM), yf.reshape(n, FOX_DIM), x2, mod,
            w_out[l].astype(BF16), norm2[l].reshape(1, d), rw, rb, seq=seq, tm=tm)

        block_e, tok_tab, slot_tab = _routing_tables(top_i, rows_per_block=expert_rows, tr=tm)
        y4 = moe_experts_fused(h2, block_e, tok_tab, slot_tab, moe_w_gu, b_gu4, moe_w_down, b_down4,
                               layer=l, r=expert_rows)
        x2 = moe_sum(y4, top_g, x1, mod, nf, seq=seq, tc=tc, final_norm=(l == depth - 1))
    return x2.reshape(batch, seq, d)


def kernel(x, c, ada_w, ada_b, norm1, w_in, w_out, rwkv_mu, rwkv_w0, rwkv_w_up, rwkv_a0, rwkv_a_up, rwkv_g_up, rwkv_k_k, rwkv_k_a, rwkv_r_k, rwkv_ln_w, rwkv_ln_b, pool_w, pool_scale, fox_b_f, norm2, router_w, router_b, moe_w_gu, moe_b_gu, moe_w_down, moe_b_down, norm_f):
    return trunk(x, c, ada_w, ada_b, norm1, w_in, w_out, rwkv_mu, rwkv_w0, rwkv_w_up, rwkv_a0, rwkv_a_up,
                 rwkv_g_up, rwkv_k_k, rwkv_k_a, rwkv_r_k, rwkv_ln_w, rwkv_ln_b, pool_w, pool_scale, fox_b_f,
                 norm2, router_w, router_b, moe_w_gu, moe_b_gu, moe_w_down, moe_b_down, norm_f)
```

```python
import functools

import jax
import jax.numpy as jnp
from jax import lax
from jax.experimental import pallas as pl
from jax.experimental.pallas import tpu as pltpu

F32 = jnp.float32
BF16 = jnp.bfloat16

D_MODEL = 1024
HEAD_DIM = 64
RWKV_DIM = 256
RWKV_HEADS = 4
RWKV_LORA_IN = 256
RWKV_GN_EPS = 64e-5
POOL_DIM = 256
POOL_WINDOWS = (2, 4, 8, 16)
POOL_HALO = 16
FOX_DIM = 512
FOX_HEADS = 8
N_EXPERTS = 32
TOP_K = 4
D_FF = 1024
SWIGLU_LIMIT = 7.0
SWIGLU_ALPHA = 1.702
NORM_EPS = 1e-6
RWKV_IN = 1024
MAIN_COLS = RWKV_IN + POOL_DIM + 3 * FOX_DIM
LANES = 128
RWKV_CHUNK = 64
NEG_BIG = -1e30
VMEM_LIMIT = 48 * 1024 * 1024
NN = (((1,), (0,)), ((), ()))
NT = (((1,), (1,)), ((), ()))
BNN = (((2,), (1,)), ((0,), (0,)))
BNT = (((2,), (2,)), ((0,), (0,)))


def _dg(a, b, dims):
    return lax.dot_general(a, b, dims, preferred_element_type=F32)


def _split2(x):
    hi = x.astype(BF16)
    lo = (x - hi.astype(F32)).astype(BF16)
    return hi, lo


def _split3(x):
    hi = x.astype(BF16)
    r = x - hi.astype(F32)
    mid = r.astype(BF16)
    lo = (r - mid.astype(F32)).astype(BF16)
    return hi, mid, lo


def _mm(a, b, dims, passes=1):
    if passes == 1:
        return _dg(a.astype(BF16), b.astype(BF16), dims)
    ah, al = _split2(a)
    bh, bl = _split2(b)
    return _dg(ah, bh, dims) + (_dg(ah, bl, dims) + _dg(al, bh, dims))


def _mm_exact_lhs(a_bf16, b, dims):
    hi, mid, lo = _split3(b)
    return _dg(a_bf16, hi, dims) + (_dg(a_bf16, mid, dims) + _dg(a_bf16, lo, dims))


def _mm_exact_rhs(a, b_bf16, dims):
    hi, mid, lo = _split3(a)
    return _dg(hi, b_bf16, dims) + (_dg(mid, b_bf16, dims) + _dg(lo, b_bf16, dims))


def _sigmoid(x):
    return 1.0 / (1.0 + jnp.exp(-x))


def _softplus(x):
    return jnp.maximum(x, 0.0) + jnp.log1p(jnp.exp(-jnp.abs(x)))


def _params(*sem):
    return pltpu.CompilerParams(dimension_semantics=sem, vmem_limit_bytes=VMEM_LIMIT)


def _ada_kernel(c_ref, w_ref, b_ref, o_ref):
    c = c_ref[...]
    cond = c * _sigmoid(c)
    o_ref[0] = _mm(cond, w_ref[0], NN, passes=3) + b_ref[0]


def ada_modulation(c, ada_w, ada_b):
    L, D, D6 = ada_w.shape
    B = c.shape[0]
    rows = 8
    cp = jnp.zeros((rows, D), F32).at[:B].set(c)
    out = pl.pallas_call(
        _ada_kernel,
        grid=(L, D6 // D),
        in_specs=[
            pl.BlockSpec((rows, D), lambda l, j: (0, 0)),
            pl.BlockSpec((1, D, D), lambda l, j: (l, 0, j)),
            pl.BlockSpec((1, 1, D), lambda l, j: (l, 0, j)),
        ],
        out_specs=pl.BlockSpec((1, rows, D), lambda l, j: (l, 0, j)),
        out_shape=jax.ShapeDtypeStruct((L, rows, D6), F32),
        compiler_params=_params("arbitrary", "arbitrary"),
        name="ada_modulation",
    )(cp, ada_w, ada_b.reshape(L, 1, D6))
    return out[:, :B].reshape(L, B, D6 // D, D)


def _rms_modulate(x, gain, scale, shift):
    ms = jnp.mean(x * x, axis=-1, keepdims=True)
    return x * lax.rsqrt(ms + NORM_EPS) * gain * (1.0 + scale) + shift


def _inproj_kernel(x_ref, mod_ref, n1_ref, wm_ref, wf_ref, bf_ref,
                   zr_ref, zp_ref, q_ref, k_ref, vt_ref, cumc_ref, carry_ref,
                   *, tiles_per_batch, tm, tv):
    i = pl.program_id(0)

    @pl.when(i % tiles_per_batch == 0)
    def _():
        carry_ref[...] = jnp.zeros_like(carry_ref)

    mod = mod_ref[0]
    h = _rms_modulate(x_ref[...], n1_ref[...], mod[1:2], mod[0:1])
    z = _dg(h.astype(BF16), wm_ref[...], NN)
    zr_ref[...] = z[:, :RWKV_IN]
    zp_ref[...] = z[:, RWKV_IN:RWKV_IN + POOL_DIM]
    o = RWKV_IN + POOL_DIM
    q_ref[...] = (z[:, o:o + FOX_DIM] * (HEAD_DIM ** -0.5)).astype(BF16)
    k_ref[...] = z[:, o + FOX_DIM:o + 2 * FOX_DIM].astype(BF16)
    for u in range(tm // tv):
        vt_ref[0, u] = z[u * tv:(u + 1) * tv, o + 2 * FOX_DIM:o + 3 * FOX_DIM].T.astype(BF16)

    fl = _mm(h, wf_ref[...], NN, passes=3) + bf_ref[...]
    logf = jnp.minimum(fl, 0.0) - jnp.log1p(jnp.exp(-jnp.abs(fl)))
    row = lax.broadcasted_iota(jnp.int32, (tm, tm), 0)
    col = lax.broadcasted_iota(jnp.int32, (tm, tm), 1)
    tri = (row >= col).astype(BF16)
    cum = _mm_exact_lhs(tri, logf, NN) + carry_ref[...]
    carry_ref[...] = cum[tm - 1:tm, :]
    cumc_ref[...] = cum


def in_projection(x2, mod, norm1, w_main, w_f, b_f, *, batch, seq, tm, tv):
    n, d = x2.shape
    tpb = seq // tm
    row = lambda i: (i, 0)
    const = lambda i: (0, 0)
    out_shapes = (
        jax.ShapeDtypeStruct((n, RWKV_IN), F32),
        jax.ShapeDtypeStruct((n, POOL_DIM), F32),
        jax.ShapeDtypeStruct((n, FOX_DIM), BF16),
        jax.ShapeDtypeStruct((n, FOX_DIM), BF16),
        jax.ShapeDtypeStruct((batch, seq // tv, FOX_DIM, tv), BF16),
        jax.ShapeDtypeStruct((n, LANES), F32),
    )
    return pl.pallas_call(
        functools.partial(_inproj_kernel, tiles_per_batch=tpb, tm=tm, tv=tv),
        grid=(n // tm,),
        in_specs=[
            pl.BlockSpec((tm, d), row),
            pl.BlockSpec((1, 6, d), lambda i: (i // tpb, 0, 0)),
            pl.BlockSpec((1, d), const),
            pl.BlockSpec((d, MAIN_COLS), const),
            pl.BlockSpec((d, LANES), const),
            pl.BlockSpec((1, LANES), const),
        ],
        out_specs=(
            pl.BlockSpec((tm, RWKV_IN), row),
            pl.BlockSpec((tm, POOL_DIM), row),
            pl.BlockSpec((tm, FOX_DIM), row),
            pl.BlockSpec((tm, FOX_DIM), row),
            pl.BlockSpec((1, tm // tv, FOX_DIM, tv), lambda i: (i // tpb, i % tpb, 0, 0)),
            pl.BlockSpec((tm, LANES), row),
        ),
        out_shape=out_shapes,
        scratch_shapes=[pltpu.VMEM((1, LANES), F32)],
        compiler_params=_params("arbitrary"),
        name="in_projection",
    )(x2, mod, norm1, w_main, w_f, b_f)


def _head_blocks(y, same_head):
    return jnp.concatenate([y] * RWKV_HEADS, axis=1) * same_head


def _mm_shared(xs, y, dims, passes):
    sizes = [x.shape[1] for x in xs]
    x = xs[0] if len(xs) == 1 else jnp.concatenate(xs, axis=1)
    m = x.shape[1]
    if passes == 1:
        out = _dg(x.astype(BF16), y.astype(BF16), dims)
    else:
        xh, xl = _split2(x)
        yh, yl = y if isinstance(y, tuple) else _split2(y)
        both = _dg(jnp.concatenate([xh, xl], axis=1), yh, dims)
        out = both[:, :m] + (both[:, m:] + _dg(xh, yl, dims))
    outs, start = [], 0
    for s in sizes:
        outs.append(out[:, start:start + s])
        start += s
    return outs


def _mm_heads(xs, y, same_head, passes, dims):
    if passes == 1:
        return _mm_shared(xs, _head_blocks(y.astype(BF16), same_head), dims, 1)
    yh, yl = _split2(y)
    return _mm_shared(xs, (_head_blocks(yh, same_head), _head_blocks(yl, same_head)), dims, passes)


def _unit_lower_inverse(a, same_head, passes):
    c = a.shape[-2]
    ri = lax.broadcasted_iota(jnp.int32, a.shape[-2:], 0)
    ci = lax.broadcasted_iota(jnp.int32, a.shape[-2:], 1) % c
    same16 = (ri // 16) == (ci // 16)
    same32 = (ri // 32) == (ci // 32)
    eye = (ri == ci).astype(F32)
    mm = lambda us, v: _mm_heads(us, v, same_head, passes, BNN)
    a16 = jnp.where(same16, a, 0.0)
    t = eye + a16
    (p,) = mm([a16], a16)
    for _ in range(2):
        tp, p2 = mm([t, p], p)
        t = t + tp
        p = p2
    t = t + mm([t], p)[0]
    o16 = jnp.where(same32 & jnp.logical_not(same16), a, 0.0)
    t = t + mm([t], mm([o16], t)[0])[0]
    o32 = jnp.where(same32, 0.0, a)
    return t + mm([t], mm([o32], t)[0])[0]


def _rwkv_kernel(z_ref, mu_ref, pv_ref, wl_ref, y_ref, prev_ref, st_ref, *, nb, nc, passes):
    c = RWKV_CHUNK
    kd = RWKV_DIM
    g = nb * nc
    rows = nc * c
    p_score, p_inv, p_apply, p_state = passes

    @pl.when(pl.program_id(0) == 0)
    def _():
        prev_ref[...] = jnp.zeros_like(prev_ref)
        st_ref[...] = jnp.zeros_like(st_ref)

    mu = mu_ref[...]
    row0 = lax.broadcasted_iota(jnp.int32, (rows, RWKV_IN), 0) == 0
    zs = []
    for b in range(nb):
        zb = z_ref[b]
        prev = jnp.where(row0, prev_ref[b], pltpu.roll(zb, 1, 0))
        prev_ref[b] = zb[rows - 1:rows, :]
        zs.append(zb + (prev - zb) * mu)
    zf = jnp.stack(zs, axis=0).reshape(g * c, RWKV_IN)

    pv = pv_ref[...]
    w0, a0, k_k, k_a, r_k, ln_w, ln_b = (pv[i:i + 1] for i in range(7))
    r = zf[:, 0:kd]
    k = zf[:, kd:2 * kd]
    v = zf[:, 2 * kd:3 * kd]
    lo = zf[:, 3 * kd:]
    lane = lax.broadcasted_iota(jnp.int32, (1, kd), 1)
    lora_in = jnp.where(lane < 64, jnp.tanh(lo), jnp.where(lane < 128, lo, _sigmoid(lo)))
    lin = _dg(lora_in.astype(BF16), wl_ref[...], NN)
    lw = -jnp.exp(-_softplus(-(w0 + lin[:, 0:kd])) - 0.5)
    a = _sigmoid(a0 + lin[:, kd:2 * kd])
    gate = lin[:, 2 * kd:]

    hrow = lax.broadcasted_iota(jnp.int32, (kd, kd), 0) // HEAD_DIM
    hcol = lax.broadcasted_iota(jnp.int32, (kd, kd), 1) // HEAD_DIM
    same_head = hrow == hcol
    ones_bd = same_head.astype(BF16)
    head_sum = lambda t: _mm_exact_rhs(t, ones_bd, NN)

    kkr = k * k_k
    kk = kkr / jnp.maximum(jnp.sqrt(head_sum(kkr * kkr)), 1e-12)
    kp = k * (1.0 + (a - 1.0) * k_a)
    bonus = head_sum(r * kp * r_k) * v

    to3 = lambda t: t.reshape(g, c, kd)
    r3, kp3, v3, kk3, a3, lw3 = (to3(t) for t in (r, kp, v, kk, a, lw))

    lower = (lax.broadcasted_iota(jnp.int32, (c, c), 0) >= lax.broadcasted_iota(jnp.int32, (c, c), 1))
    tri = jnp.broadcast_to(lower.astype(BF16)[None], (g, c, c))
    cum = _mm_exact_lhs(tri, lw3, BNN)
    cum_end = cum[:, c - 1:c, :]
    e_in = jnp.exp(cum)
    e_out = jnp.exp(-cum)
    e_tail = jnp.exp(cum_end - cum)
    p_end = jnp.exp(cum_end)
    beta = kk3 * a3
    ab = -kk3 * jnp.exp(cum - lw3)
    rb = r3 * e_in
    bb = beta * e_out
    kb = kp3 * e_out
    bt = beta * e_tail
    kt = kp3 * e_tail

    ri = lax.broadcasted_iota(jnp.int32, (c, kd), 0)
    cj = lax.broadcasted_iota(jnp.int32, (c, kd), 1) % c
    strict = ri > cj
    incl = ri >= cj
    sab, srb = _mm_heads([ab, rb], bb, ones_bd, p_score, BNT)
    sak, srk = _mm_heads([ab, rb], kb, ones_bd, p_score, BNT)
    a_ab = jnp.where(strict, sab, 0.0)
    a_rb = jnp.where(incl, srb, 0.0)
    a_ak = jnp.where(strict, sak, 0.0)
    a_rk = jnp.where(incl, srk, 0.0)
    t_inv = _unit_lower_inverse(a_ab, ones_bd, p_inv)

    app = lambda us, w: _mm_heads(us, w, ones_bd, p_apply, BNN)
    akv, arkv = app([a_ak, a_rk], v3)
    (w_all,) = app([t_inv], ab)
    (u0_all,) = app([t_inv], akv)
    rq = rb + app([a_rb], w_all)[0]
    yc = app([a_rb], u0_all)[0] + arkv

    bt_t = jnp.swapaxes(bt, 1, 2)
    kt_t = jnp.swapaxes(kt, 1, 2)
    eye = (lax.broadcasted_iota(jnp.int32, (kd, kd), 0)
           == lax.broadcasted_iota(jnp.int32, (kd, kd), 1))
    full = lambda u, w: _mm(u, w, BNN, p_apply)
    trans = jnp.where(same_head, full(bt_t, w_all), 0.0) + jnp.where(eye, p_end, 0.0)
    inject = jnp.where(same_head, full(bt_t, u0_all) + full(kt_t, v3), 0.0)

    per_chunk = lambda t, j: t.reshape((nb, nc) + t.shape[1:])[:, j]
    st = st_ref[...]
    ys = []
    for j in range(nc):
        y_st, st_new = _mm_shared([per_chunk(rq, j), per_chunk(trans, j)], st, BNN, p_state)
        ys.append(y_st + per_chunk(yc, j))
        st = st_new + per_chunk(inject, j)
    st_ref[...] = st
    y = jnp.stack(ys, axis=1)

    y2 = y.reshape(g * c, kd)
    mean = head_sum(y2) * (1.0 / HEAD_DIM)
    dv = y2 - mean
    var = head_sum(dv * dv) * (1.0 / HEAD_DIM)
    out = (dv * lax.rsqrt(var + RWKV_GN_EPS) * ln_w + ln_b + bonus) * gate
    y_ref[...] = out.reshape(nb, rows, kd).astype(y_ref.dtype)


def rwkv_mixer(z3, mu, pvec, w_lora, *, passes=(3, 3, 3, 3), chunks_per_step=2):
    b, s, _ = z3.shape
    c = RWKV_CHUNK * chunks_per_step
    return pl.pallas_call(
        functools.partial(_rwkv_kernel, nb=b, nc=chunks_per_step, passes=passes),
        grid=(s // c,),
        in_specs=[
            pl.BlockSpec((b, c, RWKV_IN), lambda t: (0, t, 0)),
            pl.BlockSpec((1, RWKV_IN), lambda t: (0, 0)),
            pl.BlockSpec((8, RWKV_DIM), lambda t: (0, 0)),
            pl.BlockSpec((RWKV_LORA_IN, 3 * RWKV_DIM), lambda t: (0, 0)),
        ],
        out_specs=pl.BlockSpec((b, c, RWKV_DIM), lambda t: (0, t, 0)),
        out_shape=jax.ShapeDtypeStruct((b, s, RWKV_DIM), BF16),
        scratch_shapes=[
            pltpu.VMEM((b, 1, RWKV_IN), F32),
            pltpu.VMEM((b, RWKV_DIM, RWKV_DIM), F32),
        ],
        compiler_params=_params("arbitrary"),
        name="rwkv_mixer",
    )(z3, mu, pvec, w_lora)


def _pool_kernel(z_ref, w_ref, sc_ref, y_ref, ext_ref, *, tp):
    t = pl.program_id(1)
    hl = POOL_HALO
    levels = len(POOL_WINDOWS)

    @pl.when(t == 0)
    def _():
        ext_ref[:, 0:hl, :] = jnp.zeros((levels, hl, POOL_DIM), F32)

    z = z_ref[0]
    sums = []
    cur = z
    for lvl, w in enumerate(POOL_WINDOWS):
        ext_ref[lvl, hl:hl + tp, :] = cur
        back = w // 2
        cur = cur + ext_ref[lvl, hl - back:hl - back + tp, :]
        sums.append(cur)
    for lvl in range(levels):
        ext_ref[lvl, 0:hl, :] = ext_ref[lvl, tp:tp + hl, :]

    lane = lax.broadcasted_iota(jnp.int32, (tp, POOL_DIM), 1)
    pos = (t * tp + lax.broadcasted_iota(jnp.int32, (tp, POOL_DIM), 0) + 1).astype(F32)
    grp = POOL_DIM // levels
    win_sum = sums[-1]
    count = jnp.minimum(pos, float(POOL_WINDOWS[-1]))
    for lvl in range(levels - 2, -1, -1):
        in_grp = lane < (lvl + 1) * grp
        win_sum = jnp.where(in_grp, sums[lvl], win_sum)
        count = jnp.where(in_grp, jnp.minimum(pos, float(POOL_WINDOWS[lvl])), count)
    pooled = win_sum / count - z
    y = _dg(pooled.astype(BF16), w_ref[...], NN) * sc_ref[...]
    y_ref[0] = y.astype(y_ref.dtype)


def pool_mixer(zp3, w_bd, scale, *, tp):
    b, s, _ = zp3.shape
    return pl.pallas_call(
        functools.partial(_pool_kernel, tp=tp),
        grid=(b, s // tp),
        in_specs=[
            pl.BlockSpec((1, tp, POOL_DIM), lambda i, t: (i, t, 0)),
            pl.BlockSpec((POOL_DIM, POOL_DIM), lambda i, t: (0, 0)),
            pl.BlockSpec((1, POOL_DIM), lambda i, t: (0, 0)),
        ],
        out_specs=pl.BlockSpec((1, tp, POOL_DIM), lambda i, t: (i, t, 0)),
        out_shape=jax.ShapeDtypeStruct((b, s, POOL_DIM), BF16),
        scratch_shapes=[pltpu.VMEM((len(POOL_WINDOWS), POOL_HALO + tp, POOL_DIM), F32)],
        compiler_params=_params("arbitrary", "arbitrary"),
        name="pool_mixer",
    )(zp3, w_bd, scale)


def _fox_kernel(q_ref, k_ref, vt_ref, cc_ref, o_ref, ckb_ref, m_ref, l_ref, acc_ref, sa_ref, sb_ref,
                *, tq, nblk):
    hp = pl.program_id(1)
    i = pl.program_id(2)
    lane = lax.broadcasted_iota(jnp.int32, (1, LANES), 1)
    reps = tq // LANES

    @pl.when(i == 0)
    def _():
        def fill(jb, carry):
            start = pl.multiple_of(jb * tq, tq)
            blk = cc_ref[0, pl.ds(start, tq), :]
            for hh in range(2):
                col = jnp.sum(jnp.where(lane == 2 * hp + hh, blk, 0.0), axis=-1, keepdims=True)
                ckb_ref[hh, pl.ds(start, tq), :] = jnp.broadcast_to(col, (tq, LANES))
            return carry

        lax.fori_loop(0, nblk, fill, 0)

    q2 = q_ref[0]
    m_ref[...] = jnp.full(m_ref.shape, NEG_BIG, F32)
    l_ref[...] = jnp.zeros(l_ref.shape, F32)
    acc_ref[...] = jnp.zeros(acc_ref.shape, F32)
    qh = [jnp.where(lane // HEAD_DIM == hh, q2, jnp.zeros_like(q2)) for hh in range(2)]

    def scores_into(j, dst):
        k2 = k_ref[0, pl.ds(pl.multiple_of(j * tq, tq), tq), :]
        for hh in range(2):
            dst[hh] = _dg(k2, qh[hh], NT)

    def consume(j, src, masked):
        start = pl.multiple_of(j * tq, tq)
        for hh in range(2):
            ck = ckb_ref[hh, pl.ds(start, tq), :]
            s = src[hh] - jnp.concatenate([ck] * reps, axis=1)
            if masked:
                kpos = lax.broadcasted_iota(jnp.int32, (tq, tq), 0)
                qpos = lax.broadcasted_iota(jnp.int32, (tq, tq), 1)
                s = jnp.where(kpos <= qpos, s, NEG_BIG)
            m_old = m_ref[hh][0:1, :]
            m_new = jnp.maximum(m_old, jnp.max(s, axis=0, keepdims=True))
            alpha = jnp.exp(m_old - m_new)
            p = jnp.exp(s - m_new)
            l_new = alpha * l_ref[hh][0:1, :] + jnp.sum(p, axis=0, keepdims=True)
            vth = vt_ref[0, j, hh * HEAD_DIM:(hh + 1) * HEAD_DIM, :]
            acc_ref[hh] = alpha * acc_ref[hh] + _dg(vth, p.astype(BF16), NN)
            l_ref[hh] = jnp.broadcast_to(l_new, (8, tq))
            m_ref[hh] = jnp.broadcast_to(m_new, (8, tq))

    scores_into(0, sa_ref)

    def pair(jj, carry):
        j = 2 * jj
        scores_into(j + 1, sb_ref)
        consume(j, sa_ref, False)
        scores_into(j + 2, sa_ref)
        consume(j + 1, sb_ref, False)
        return carry

    lax.fori_loop(0, i // 2, pair, 0)

    @pl.when(i % 2 == 0)
    def _():
        consume(i, sa_ref, True)

    @pl.when(i % 2 == 1)
    def _():
        scores_into(i, sb_ref)
        consume(i - 1, sa_ref, False)
        consume(i, sb_ref, True)

    ot = jnp.concatenate([acc_ref[hh] / l_ref[hh][0:1, :] for hh in range(2)], axis=0)
    o_ref[0] = ot.T.astype(o_ref.dtype)


def fox_attention(q3, k3, vt4, cumc3, *, tq):
    b, s, _ = q3.shape
    nq = s // tq
    pairs = FOX_HEADS // 2
    return pl.pallas_call(
        functools.partial(_fox_kernel, tq=tq, nblk=nq),
        grid=(b, pairs, nq),
        in_specs=[
            pl.BlockSpec((1, tq, LANES), lambda bi, hp, i: (bi, i, hp)),
            pl.BlockSpec((1, s, LANES), lambda bi, hp, i: (bi, 0, hp)),
            pl.BlockSpec((1, nq, LANES, tq), lambda bi, hp, i: (bi, 0, hp, 0)),
            pl.BlockSpec((1, s, LANES), lambda bi, hp, i: (bi, 0, 0)),
        ],
        out_specs=pl.BlockSpec((1, tq, LANES), lambda bi, hp, i: (bi, i, hp)),
        out_shape=jax.ShapeDtypeStruct((b, s, FOX_DIM), BF16),
        scratch_shapes=[
            pltpu.VMEM((2, s, LANES), F32),
            pltpu.VMEM((2, 8, tq), F32),
            pltpu.VMEM((2, 8, tq), F32),
            pltpu.VMEM((2, HEAD_DIM, tq), F32),
            pltpu.VMEM((2, tq, tq), F32),
            pltpu.VMEM((2, tq, tq), F32),
        ],
        compiler_params=_params("arbitrary", "arbitrary", "arbitrary"),
        name="fox_attention",
    )(q3, k3, vt4, cumc3)


def _outproj_kernel(yr_ref, yp_ref, yf_ref, x_ref, mod_ref, wo_ref, n2_ref, rw_ref, rb_ref,
                    x1_ref, h2_ref, ti_ref, tg_ref):
    mod = mod_ref[0]
    mixed = (_dg(yr_ref[...], wo_ref[0:RWKV_DIM, :], NN)
             + _dg(yp_ref[...], wo_ref[RWKV_DIM:RWKV_DIM + POOL_DIM, :], NN)
             + _dg(yf_ref[...], wo_ref[RWKV_DIM + POOL_DIM:, :], NN))
    x1 = x_ref[...] + mod[2:3] * mixed
    x1_ref[...] = x1
    h2 = _rms_modulate(x1, n2_ref[...], mod[4:5], mod[3:4])
    h2_ref[...] = h2

    tm = x1.shape[0]
    lane = lax.broadcasted_iota(jnp.int32, (tm, LANES), 1)
    lane_f = lane.astype(F32)
    logits = _mm(h2, rw_ref[...], NN, passes=3) + rb_ref[...]
    work = jnp.where(lane < N_EXPERTS, logits, NEG_BIG)
    top_i = jnp.zeros((tm, LANES), F32)
    top_e = jnp.zeros((tm, LANES), F32)
    v0 = None
    for kk in range(TOP_K):
        vmax = jnp.max(work, axis=-1, keepdims=True)
        idx = jnp.min(jnp.where(work == vmax, lane_f, float(LANES)), axis=-1, keepdims=True)
        if kk == 0:
            v0 = vmax
        top_i = jnp.where(lane == kk, idx, top_i)
        top_e = jnp.where(lane == kk, jnp.exp(vmax - v0), top_e)
        work = jnp.where(lane_f == idx, NEG_BIG, work)
    ti_ref[...] = top_i.astype(jnp.int32)
    tg_ref[...] = top_e / jnp.sum(top_e, axis=-1, keepdims=True)


def out_projection(yr, yp, yf, x2, mod, w_out, norm2, rw, rb, *, seq, tm):
    n, d = x2.shape
    tpb = seq // tm
    row = lambda i: (i, 0)
    const = lambda i: (0, 0)
    return pl.pallas_call(
        _outproj_kernel,
        grid=(n // tm,),
        in_specs=[
            pl.BlockSpec((tm, RWKV_DIM), row),
            pl.BlockSpec((tm, POOL_DIM), row),
            pl.BlockSpec((tm, FOX_DIM), row),
            pl.BlockSpec((tm, d), row),
            pl.BlockSpec((1, 6, d), lambda i: (i // tpb, 0, 0)),
            pl.BlockSpec((d, d), const),
            pl.BlockSpec((1, d), const),
            pl.BlockSpec((d, LANES), const),
            pl.BlockSpec((1, LANES), const),
        ],
        out_specs=(
            pl.BlockSpec((tm, d), row),
            pl.BlockSpec((tm, d), row),
            pl.BlockSpec((tm, LANES), row),
            pl.BlockSpec((tm, LANES), row),
        ),
        out_shape=(
            jax.ShapeDtypeStruct((n, d), F32),
            jax.ShapeDtypeStruct((n, d), F32),
            jax.ShapeDtypeStruct((n, LANES), jnp.int32),
            jax.ShapeDtypeStruct((n, LANES), F32),
        ),
        compiler_params=_params("arbitrary"),
        name="out_projection",
    )(yr, yp, yf, x2, mod, w_out, norm2, rw, rb)


def _rank_kernel(ti_ref, rank_ref, cnt_ref, carry_ref, *, tr):
    @pl.when(pl.program_id(0) == 0)
    def _():
        carry_ref[...] = jnp.zeros_like(carry_ref)

    ti = ti_ref[...]
    lane = lax.broadcasted_iota(jnp.int32, (tr, LANES), 1)
    hits = [ti[:, kk:kk + 1] == lane for kk in range(TOP_K)]
    cnt = hits[0].astype(F32)
    for kk in range(1, TOP_K):
        cnt = cnt + hits[kk].astype(F32)
    before = (lax.broadcasted_iota(jnp.int32, (tr, tr), 0)
              > lax.broadcasted_iota(jnp.int32, (tr, tr), 1)).astype(BF16)
    pre = _dg(before, cnt.astype(BF16), NN) + carry_ref[...]
    rank = jnp.zeros((tr, LANES), F32)
    for kk in range(TOP_K):
        rk = jnp.sum(jnp.where(hits[kk], pre, 0.0), axis=-1, keepdims=True)
        rank = jnp.where(lane == kk, rk, rank)
    rank_ref[...] = rank.astype(jnp.int32)
    total = carry_ref[...] + jnp.sum(cnt, axis=0, keepdims=True)
    carry_ref[...] = total
    cnt_ref[...] = jnp.broadcast_to(total, cnt_ref.shape)


def moe_rank(top_i, *, tr):
    n = top_i.shape[0]
    return pl.pallas_call(
        functools.partial(_rank_kernel, tr=tr),
        grid=(n // tr,),
        in_specs=[pl.BlockSpec((tr, LANES), lambda i: (i, 0))],
        out_specs=(pl.BlockSpec((tr, LANES), lambda i: (i, 0)),
                   pl.BlockSpec((8, LANES), lambda i: (0, 0))),
        out_shape=(jax.ShapeDtypeStruct((n, LANES), jnp.int32),
                   jax.ShapeDtypeStruct((8, LANES), F32)),
        scratch_shapes=[pltpu.VMEM((1, LANES), F32)],
        compiler_params=_params("arbitrary"),
        name="moe_rank",
    )(top_i)


def _routing_tables(top_i, *, rows_per_block, tr):
    n = top_i.shape[0]
    r = rows_per_block
    rank, cnt = moe_rank(top_i, tr=tr)
    counts = cnt[0, :N_EXPERTS].astype(jnp.int32)
    padded = (counts + r - 1) // r * r
    pend = jnp.cumsum(padded)
    pstart = pend - padded
    e_sel = top_i[:, :TOP_K]
    onehot = e_sel[:, :, None] == jnp.arange(N_EXPERTS, dtype=jnp.int32)[None, None, :]
    dest = (jnp.sum(jnp.where(onehot, pstart[None, None, :], 0), axis=-1) + rank[:, :TOP_K]).astype(jnp.int32)
    nb = -(-(n * TOP_K + N_EXPERTS * (r - 1)) // r)
    bstart = jnp.arange(nb, dtype=jnp.int32) * r
    block_e = jnp.minimum(jnp.sum(bstart[:, None] >= pend[None, :], axis=1), N_EXPERTS - 1).astype(jnp.int32)
    in_e = block_e[:, None] == jnp.arange(N_EXPERTS, dtype=jnp.int32)[None, :]
    seg_end = jnp.sum(jnp.where(in_e, (pstart + counts)[None, :], 0), axis=1)
    nvalid = jnp.clip(seg_end - bstart, 0, r).astype(jnp.int32)
    nused = (pend[-1] // r).astype(jnp.int32).reshape(1)
    pad = jnp.stack([pstart + counts, padded - counts]).astype(jnp.int32)
    return dest, block_e, nvalid, nused, pad, nb


def _zero_fill(pad_ref, nu_ref, xs_ref, zero_ref, zsem, *, r, nb):
    row_sem, block_sem = zsem.at[0], zsem.at[1]
    zero_ref[...] = jnp.zeros_like(zero_ref)

    def each_copy(fn):
        def per_expert(e, carry):
            start = pad_ref[0, e]

            def per_row(i, c2):
                fn(pltpu.make_async_copy(zero_ref.at[pl.ds(0, 1)], xs_ref.at[pl.ds(start + i, 1)], row_sem))
                return c2

            lax.fori_loop(0, pad_ref[1, e], per_row, 0)
            return carry

        lax.fori_loop(0, N_EXPERTS, per_expert, 0)

        def per_block(b, carry):
            fn(pltpu.make_async_copy(zero_ref, xs_ref.at[pl.ds(pl.multiple_of(b * r, r), r)], block_sem))
            return carry

        lax.fori_loop(nu_ref[0], nb, per_block, 0)

    each_copy(lambda cp: cp.start())
    each_copy(lambda cp: cp.wait())


def _dispatch_kernel(dest_ref, pad_ref, nu_ref, h_ref, xs_ref, zero_ref, sem, zsem, *, td, r, nb):
    @pl.when(pl.program_id(0) == 0)
    def _():
        _zero_fill(pad_ref, nu_ref, xs_ref, zero_ref, zsem, r=r, nb=nb)

    for t in range(td):
        for kk in range(TOP_K):
            pltpu.make_async_copy(h_ref.at[pl.ds(t, 1)],
                                  xs_ref.at[pl.ds(dest_ref[0, 0, TOP_K * t + kk], 1)], sem
                                  ).start(priority=kk % 2)
    for kk in range(TOP_K):
        pltpu.make_async_copy(h_ref, xs_ref.at[pl.ds(0, td)], sem).wait()


def moe_dispatch(h2, dest2, pad, nused, *, nb, r, td):
    n, d = h2.shape
    return pl.pallas_call(
        functools.partial(_dispatch_kernel, td=td, r=r, nb=nb),
        grid=(n // td,),
        in_specs=[
            pl.BlockSpec((1, 1, TOP_K * td), lambda i: (i, 0, 0), memory_space=pltpu.SMEM),
            pl.BlockSpec(memory_space=pltpu.SMEM),
            pl.BlockSpec(memory_space=pltpu.SMEM),
            pl.BlockSpec((td, d), lambda i: (i, 0)),
        ],
        out_specs=pl.BlockSpec(memory_space=pl.ANY),
        out_shape=jax.ShapeDtypeStruct((nb * r, d), F32),
        scratch_shapes=[pltpu.VMEM((r, d), F32), pltpu.SemaphoreType.DMA, pltpu.SemaphoreType.DMA((2,))],
        compiler_params=_params("arbitrary"),
        name="moe_dispatch",
    )(dest2, pad, nused, h2)


def _expert_kernel(be_ref, nv_ref, nu_ref, x_ref, wgu_ref, bgu_ref, wd_ref, bd_ref, y_ref,
                   wgu_bf, wd_bf, cur_ref):
    b = pl.program_id(0)
    e = be_ref[b]
    nv = nv_ref[b]

    @pl.when(b == 0)
    def _():
        cur_ref[0] = -1

    @pl.when((nv > 0) & (cur_ref[0] != e))
    def _():
        cur_ref[0] = e
        step = 128

        def cast(i, carry):
            s = pl.multiple_of(i * step, step)
            wgu_bf[pl.ds(s, step), :] = wgu_ref[0, 0, pl.ds(s, step), :].astype(BF16)
            wd_bf[pl.ds(s, step), :] = wd_ref[0, 0, pl.ds(s, step), :].astype(BF16)
            return carry

        lax.fori_loop(0, D_MODEL // step, cast, 0)

    @pl.when(nv == 0)
    def _():
        y_ref[...] = jnp.zeros_like(y_ref)

    @pl.when(nv > 0)
    def _():
        x = x_ref[...].astype(BF16)
        gu = _dg(x, wgu_bf[...], NN) + bgu_ref[0, 0]
        gate = jnp.minimum(gu[:, :D_FF], SWIGLU_LIMIT)
        up = jnp.clip(gu[:, D_FF:], -SWIGLU_LIMIT, SWIGLU_LIMIT)
        act = (up + 1.0) * (gate * _sigmoid(SWIGLU_ALPHA * gate))
        y_ref[...] = _dg(act.astype(BF16), wd_bf[...], NN) + bd_ref[0, 0]


def moe_experts(xs, block_e, nvalid, nused, w_gu, b_gu, w_down, b_down, *, layer, r):
    rows, d = xs.shape
    nb = rows // r
    blk = lambda b, be, nv, nu: (jnp.minimum(b, nu[0] - 1), 0)
    wsel = lambda b, be, nv, nu: (layer, be[b], 0, 0)
    grid_spec = pltpu.PrefetchScalarGridSpec(
        num_scalar_prefetch=3,
        grid=(nb,),
        in_specs=[
            pl.BlockSpec((r, d), blk),
            pl.BlockSpec((1, 1, d, 2 * D_FF), wsel),
            pl.BlockSpec((1, 1, 1, 2 * D_FF), wsel),
            pl.BlockSpec((1, 1, D_FF, d), wsel),
            pl.BlockSpec((1, 1, 1, d), wsel),
        ],
        out_specs=pl.BlockSpec((r, d), lambda b, be, nv, nu: (b, 0)),
        scratch_shapes=[
            pltpu.VMEM((d, 2 * D_FF), BF16),
            pltpu.VMEM((D_FF, d), BF16),
            pltpu.SMEM((1,), jnp.int32),
        ],
    )
    return pl.pallas_call(
        _expert_kernel,
        grid_spec=grid_spec,
        out_shape=jax.ShapeDtypeStruct((rows, d), F32),
        compiler_params=_params("arbitrary"),
        name="moe_experts",
    )(block_e, nvalid, nused, xs, w_gu, b_gu, w_down, b_down)


def _combine_kernel(dest0_ref, destn_ref, yb_ref, tg_ref, x_ref, mod_ref, nf_ref, o_ref, buf0, buf1, sem,
                    *, tc, nt, final_norm):
    i = pl.program_id(0)
    bufs = (buf0, buf1)

    def gather(tab_ref, half):
        for t in range(tc):
            for kk in range(TOP_K):
                pltpu.make_async_copy(yb_ref.at[pl.ds(tab_ref[0, 0, TOP_K * t + kk], 1)],
                                      bufs[half].at[kk, pl.ds(t, 1)], sem.at[half]).start(priority=kk % 2)

    def gather_wait(half):
        for kk in range(TOP_K):
            pltpu.make_async_copy(yb_ref.at[pl.ds(0, tc)], bufs[half].at[kk], sem.at[half]).wait()

    @pl.when(i == 0)
    def _():
        gather(dest0_ref, 0)

    def step(cur):
        @pl.when(i + 1 < nt)
        def _():
            gather(destn_ref, 1 - cur)

        gather_wait(cur)
        _combine_tile(bufs[cur], tg_ref, x_ref, mod_ref, nf_ref, o_ref, final_norm=final_norm)

    for parity in range(2):
        pl.when(i % 2 == parity)(functools.partial(step, parity))


def _combine_tile(buf, tg_ref, x_ref, mod_ref, nf_ref, o_ref, *, final_norm):
    tg = tg_ref[...]
    y = tg[:, 0:1] * buf[0]
    for kk in range(1, TOP_K):
        y = y + tg[:, kk:kk + 1] * buf[kk]
    x = x_ref[...] + mod_ref[0][5:6] * y
    if final_norm:
        ms = jnp.mean(x * x, axis=-1, keepdims=True)
        x = x * lax.rsqrt(ms + NORM_EPS) * nf_ref[...]
    o_ref[...] = x


def moe_combine(yb, dest2, tg, x1, mod, norm_f, *, seq, tc, final_norm):
    n, d = x1.shape
    tpb = seq // tc
    nt = n // tc
    row = lambda i: (i, 0)
    return pl.pallas_call(
        functools.partial(_combine_kernel, tc=tc, nt=nt, final_norm=final_norm),
        grid=(nt,),
        in_specs=[
            pl.BlockSpec((1, 1, TOP_K * tc), lambda i: (0, 0, 0), memory_space=pltpu.SMEM),
            pl.BlockSpec((1, 1, TOP_K * tc), lambda i: (jnp.minimum(i + 1, nt - 1), 0, 0),
                         memory_space=pltpu.SMEM),
            pl.BlockSpec(memory_space=pl.ANY),
            pl.BlockSpec((tc, LANES), row),
            pl.BlockSpec((tc, d), row),
            pl.BlockSpec((1, 6, d), lambda i: (i // tpb, 0, 0)),
            pl.BlockSpec((1, d), lambda i: (0, 0)),
        ],
        out_specs=pl.BlockSpec((tc, d), row),
        out_shape=jax.ShapeDtypeStruct((n, d), F32),
        scratch_shapes=[pltpu.VMEM((TOP_K, tc, d), F32), pltpu.VMEM((TOP_K, tc, d), F32),
                        pltpu.SemaphoreType.DMA((2,))],
        compiler_params=_params("arbitrary"),
        name="moe_combine",
    )(dest2, dest2, yb, tg, x1, mod, norm_f)


def _block_diag(blocks):
    g, m, _ = blocks.shape
    out = jnp.zeros((g * m, g * m), blocks.dtype)
    for i in range(g):
        out = out.at[i * m:(i + 1) * m, i * m:(i + 1) * m].set(blocks[i])
    return out


def _pad_cols(w, cols):
    return jnp.zeros(w.shape[:-1] + (cols,), w.dtype).at[..., :w.shape[-1]].set(w)


def trunk(x, c, ada_w, ada_b, norm1, w_in, w_out, rwkv_mu, rwkv_w0, rwkv_w_up, rwkv_a0, rwkv_a_up,
          rwkv_g_up, rwkv_k_k, rwkv_k_a, rwkv_r_k, rwkv_ln_w, rwkv_ln_b, pool_w, pool_scale, fox_b_f,
          norm2, router_w, router_b, moe_w_gu, moe_b_gu, moe_w_down, moe_b_down, norm_f,
          *, tm=512, tp=512, tq=512, expert_rows=256, td=256, tc=256, rwkv_passes=(1, 3, 1, 3)):
    batch, seq, d = x.shape
    depth = ada_w.shape[0]
    n = batch * seq
    mod_all = ada_modulation(c, ada_w, ada_b)
    x2 = x.reshape(n, d)
    nf = norm_f.reshape(1, d)
    b_gu4 = moe_b_gu.reshape(depth, N_EXPERTS, 1, 2 * D_FF)
    b_down4 = moe_b_down.reshape(depth, N_EXPERTS, 1, d)
    for l in range(depth):
        mod = mod_all[l]
        w_main = w_in[l][:, :MAIN_COLS].astype(BF16)
        w_f = _pad_cols(w_in[l][:, MAIN_COLS:], LANES)
        b_f = _pad_cols(fox_b_f[l].reshape(1, FOX_HEADS), LANES)
        w_lora = jnp.zeros((RWKV_LORA_IN, 3 * RWKV_DIM), F32)
        w_lora = w_lora.at[0:64, 0:RWKV_DIM].set(rwkv_w_up[l])
        w_lora = w_lora.at[64:128, RWKV_DIM:2 * RWKV_DIM].set(rwkv_a_up[l])
        w_lora = w_lora.at[128:256, 2 * RWKV_DIM:].set(rwkv_g_up[l]).astype(BF16)
        pvec = jnp.stack([rwkv_w0[l], rwkv_a0[l], rwkv_k_k[l], rwkv_k_a[l], rwkv_r_k[l].reshape(-1),
                          rwkv_ln_w[l], rwkv_ln_b[l], jnp.zeros((RWKV_DIM,), F32)], axis=0)
        w_pool = _block_diag(pool_w[l]).astype(BF16)
        rw = _pad_cols(router_w[l], LANES)
        rb = _pad_cols(router_b[l].reshape(1, N_EXPERTS), LANES)

        zr, zp, q, k, vt, cumc = in_projection(
            x2, mod, norm1[l].reshape(1, d), w_main, w_f, b_f, batch=batch, seq=seq, tm=tm, tv=tq)
        yr = rwkv_mixer(zr.reshape(batch, seq, RWKV_IN), rwkv_mu[l].reshape(1, RWKV_IN), pvec, w_lora,
                        passes=rwkv_passes)
        yp = pool_mixer(zp.reshape(batch, seq, POOL_DIM), w_pool, pool_scale[l].reshape(1, POOL_DIM), tp=tp)
        to3 = lambda t: t.reshape(batch, seq, t.shape[-1])
        yf = fox_attention(to3(q), to3(k), vt, to3(cumc), tq=tq)
        x1, h2, top_i, top_g = out_projection(
            yr.reshape(n, RWKV_DIM), yp.reshape(n, POOL_DIM), yf.reshape(n, FOX_DIM), x2, mod,
            w_out[l].astype(BF16), norm2[l].reshape(1, d), rw, rb, seq=seq, tm=tm)

        dest, block_e, nvalid, nused, pad, nb = _routing_tables(top_i, rows_per_block=expert_rows, tr=tm)
        xs = moe_dispatch(h2, dest.reshape(n // td, 1, TOP_K * td), pad, nused, nb=nb, r=expert_rows, td=td)
        yb = moe_experts(xs, block_e, nvalid, nused, moe_w_gu, b_gu4, moe_w_down, b_down4,
                         layer=l, r=expert_rows)
        x2 = moe_combine(yb, dest.reshape(n // tc, 1, TOP_K * tc), top_g, x1, mod, nf, seq=seq, tc=tc,
                         final_norm=(l == depth - 1))
    return x2.reshape(batch, seq, d)


def kernel(x, c, ada_w, ada_b, norm1, w_in, w_out, rwkv_mu, rwkv_w0, rwkv_w_up, rwkv_a0, rwkv_a_up, rwkv_g_up, rwkv_k_k, rwkv_k_a, rwkv_r_k, rwkv_ln_w, rwkv_ln_b, pool_w, pool_scale, fox_b_f, norm2, router_w, router_b, moe_w_gu, moe_b_gu, moe_w_down, moe_b_down, norm_f):
    return trunk(x, c, ada_w, ada_b, norm1, w_in, w_out, rwkv_mu, rwkv_w0, rwkv_w_up, rwkv_a0, rwkv_a_up,
                 rwkv_g_up, rwkv_k_k, rwkv_k_a, rwkv_r_k, rwkv_ln_w, rwkv_ln_b, pool_w, pool_scale, fox_b_f,
                 norm2, router_w, router_b, moe_w_gu, moe_b_gu, moe_w_down, moe_b_down, norm_f)
```

```python
import functools

import jax
import jax.numpy as jnp
from jax import lax
from jax.experimental import pallas as pl
from jax.experimental.pallas import tpu as pltpu

F32 = jnp.float32
BF16 = jnp.bfloat16

D_MODEL = 1024
HEAD_DIM = 64
RWKV_DIM = 256
RWKV_HEADS = 4
RWKV_LORA_IN = 256
RWKV_GN_EPS = 64e-5
POOL_DIM = 256
POOL_WINDOWS = (2, 4, 8, 16)
POOL_HALO = 16
FOX_DIM = 512
FOX_HEADS = 8
N_EXPERTS = 32
TOP_K = 4
D_FF = 1024
SWIGLU_LIMIT = 7.0
SWIGLU_ALPHA = 1.702
NORM_EPS = 1e-6
RWKV_IN = 1024
MAIN_COLS = RWKV_IN + POOL_DIM + 3 * FOX_DIM
LANES = 128
RWKV_CHUNK = 64
NEG_BIG = -1e30
LOG2_E = 1.4426950408889634
VMEM_LIMIT = 48 * 1024 * 1024
NN = (((1,), (0,)), ((), ()))
NT = (((1,), (1,)), ((), ()))
BNN = (((2,), (1,)), ((0,), (0,)))
BNT = (((2,), (2,)), ((0,), (0,)))


def _dg(a, b, dims):
    return lax.dot_general(a, b, dims, preferred_element_type=F32)


def _split2(x):
    hi = x.astype(BF16)
    lo = (x - hi.astype(F32)).astype(BF16)
    return hi, lo


def _split3(x):
    hi = x.astype(BF16)
    r = x - hi.astype(F32)
    mid = r.astype(BF16)
    lo = (r - mid.astype(F32)).astype(BF16)
    return hi, mid, lo


def _mm(a, b, dims, passes=1):
    if passes == 1:
        return _dg(a.astype(BF16), b.astype(BF16), dims)
    ah, al = _split2(a)
    bh, bl = _split2(b)
    return _dg(ah, bh, dims) + (_dg(ah, bl, dims) + _dg(al, bh, dims))


def _mm_exact_lhs(a_bf16, b, dims):
    hi, mid, lo = _split3(b)
    return _dg(a_bf16, hi, dims) + (_dg(a_bf16, mid, dims) + _dg(a_bf16, lo, dims))


def _mm_exact_rhs(a, b_bf16, dims):
    hi, mid, lo = _split3(a)
    return _dg(hi, b_bf16, dims) + (_dg(mid, b_bf16, dims) + _dg(lo, b_bf16, dims))


def _sigmoid(x):
    return 1.0 / (1.0 + jnp.exp(-x))


def _softplus(x):
    return jnp.maximum(x, 0.0) + jnp.log1p(jnp.exp(-jnp.abs(x)))


def _params(*sem):
    return pltpu.CompilerParams(dimension_semantics=sem, vmem_limit_bytes=VMEM_LIMIT)


def _ada_kernel(c_ref, w_ref, b_ref, o_ref):
    c = c_ref[...]
    cond = c * _sigmoid(c)
    o_ref[0] = _mm(cond, w_ref[0], NN, passes=3) + b_ref[0]


def ada_modulation(c, ada_w, ada_b):
    L, D, D6 = ada_w.shape
    B = c.shape[0]
    rows = 8
    cp = jnp.zeros((rows, D), F32).at[:B].set(c)
    out = pl.pallas_call(
        _ada_kernel,
        grid=(L, D6 // D),
        in_specs=[
            pl.BlockSpec((rows, D), lambda l, j: (0, 0)),
            pl.BlockSpec((1, D, D), lambda l, j: (l, 0, j)),
            pl.BlockSpec((1, 1, D), lambda l, j: (l, 0, j)),
        ],
        out_specs=pl.BlockSpec((1, rows, D), lambda l, j: (l, 0, j)),
        out_shape=jax.ShapeDtypeStruct((L, rows, D6), F32),
        compiler_params=_params("arbitrary", "arbitrary"),
        name="ada_modulation",
    )(cp, ada_w, ada_b.reshape(L, 1, D6))
    return out[:, :B].reshape(L, B, D6 // D, D)


def _rms_modulate(x, gain, scale, shift):
    ms = jnp.mean(x * x, axis=-1, keepdims=True)
    return x * lax.rsqrt(ms + NORM_EPS) * gain * (1.0 + scale) + shift


def _inproj_kernel(x_ref, mod_ref, n1_ref, wm_ref, wf_ref, bf_ref,
                   zr_ref, zp_ref, q_ref, k_ref, vt_ref, cumc_ref, carry_ref,
                   *, tiles_per_batch, tm, tv):
    i = pl.program_id(0)

    @pl.when(i % tiles_per_batch == 0)
    def _():
        carry_ref[...] = jnp.zeros_like(carry_ref)

    mod = mod_ref[0]
    h = _rms_modulate(x_ref[...], n1_ref[...], mod[1:2], mod[0:1])
    z = _dg(h.astype(BF16), wm_ref[...], NN)
    zr_ref[...] = z[:, :RWKV_IN]
    zp_ref[...] = z[:, RWKV_IN:RWKV_IN + POOL_DIM]
    o = RWKV_IN + POOL_DIM
    q_ref[...] = (z[:, o:o + FOX_DIM] * (HEAD_DIM ** -0.5 * LOG2_E)).astype(BF16)
    k_ref[...] = z[:, o + FOX_DIM:o + 2 * FOX_DIM].astype(BF16)
    for u in range(tm // tv):
        vt_ref[0, u] = z[u * tv:(u + 1) * tv, o + 2 * FOX_DIM:o + 3 * FOX_DIM].T.astype(BF16)

    fl = _mm(h, wf_ref[...], NN, passes=3) + bf_ref[...]
    logf = jnp.minimum(fl, 0.0) - jnp.log1p(jnp.exp(-jnp.abs(fl)))
    row = lax.broadcasted_iota(jnp.int32, (tm, tm), 0)
    col = lax.broadcasted_iota(jnp.int32, (tm, tm), 1)
    tri = (row >= col).astype(BF16)
    cum = _mm_exact_lhs(tri, logf, NN) + carry_ref[...]
    carry_ref[...] = cum[tm - 1:tm, :]
    cumc_ref[...] = cum


def in_projection(x2, mod, norm1, w_main, w_f, b_f, *, batch, seq, tm, tv):
    n, d = x2.shape
    tpb = seq // tm
    row = lambda i: (i, 0)
    const = lambda i: (0, 0)
    out_shapes = (
        jax.ShapeDtypeStruct((n, RWKV_IN), F32),
        jax.ShapeDtypeStruct((n, POOL_DIM), F32),
        jax.ShapeDtypeStruct((n, FOX_DIM), BF16),
        jax.ShapeDtypeStruct((n, FOX_DIM), BF16),
        jax.ShapeDtypeStruct((batch, seq // tv, FOX_DIM, tv), BF16),
        jax.ShapeDtypeStruct((n, LANES), F32),
    )
    return pl.pallas_call(
        functools.partial(_inproj_kernel, tiles_per_batch=tpb, tm=tm, tv=tv),
        grid=(n // tm,),
        in_specs=[
            pl.BlockSpec((tm, d), row),
            pl.BlockSpec((1, 6, d), lambda i: (i // tpb, 0, 0)),
            pl.BlockSpec((1, d), const),
            pl.BlockSpec((d, MAIN_COLS), const),
            pl.BlockSpec((d, LANES), const),
            pl.BlockSpec((1, LANES), const),
        ],
        out_specs=(
            pl.BlockSpec((tm, RWKV_IN), row),
            pl.BlockSpec((tm, POOL_DIM), row),
            pl.BlockSpec((tm, FOX_DIM), row),
            pl.BlockSpec((tm, FOX_DIM), row),
            pl.BlockSpec((1, tm // tv, FOX_DIM, tv), lambda i: (i // tpb, i % tpb, 0, 0)),
            pl.BlockSpec((tm, LANES), row),
        ),
        out_shape=out_shapes,
        scratch_shapes=[pltpu.VMEM((1, LANES), F32)],
        compiler_params=_params("arbitrary"),
        name="in_projection",
    )(x2, mod, norm1, w_main, w_f, b_f)


def _head_blocks(y, same_head):
    return jnp.concatenate([y] * RWKV_HEADS, axis=1) * same_head


def _mm_shared(xs, y, dims, passes):
    sizes = [x.shape[1] for x in xs]
    x = xs[0] if len(xs) == 1 else jnp.concatenate(xs, axis=1)
    m = x.shape[1]
    if passes == 1:
        out = _dg(x.astype(BF16), y.astype(BF16), dims)
    else:
        xh, xl = _split2(x)
        yh, yl = y if isinstance(y, tuple) else _split2(y)
        both = _dg(jnp.concatenate([xh, xl], axis=1), yh, dims)
        out = both[:, :m] + (both[:, m:] + _dg(xh, yl, dims))
    outs, start = [], 0
    for s in sizes:
        outs.append(out[:, start:start + s])
        start += s
    return outs


def _mm_heads(xs, y, same_head, passes, dims):
    if passes == 1:
        return _mm_shared(xs, _head_blocks(y.astype(BF16), same_head), dims, 1)
    yh, yl = _split2(y)
    return _mm_shared(xs, (_head_blocks(yh, same_head), _head_blocks(yl, same_head)), dims, passes)


def _unit_lower_inverse(a, same_head, passes):
    c = a.shape[-2]
    ri = lax.broadcasted_iota(jnp.int32, a.shape[-2:], 0)
    ci = lax.broadcasted_iota(jnp.int32, a.shape[-2:], 1) % c
    same16 = (ri // 16) == (ci // 16)
    same32 = (ri // 32) == (ci // 32)
    eye = (ri == ci).astype(F32)
    mm = lambda us, v: _mm_heads(us, v, same_head, passes, BNN)
    a16 = jnp.where(same16, a, 0.0)
    t = eye + a16
    (p,) = mm([a16], a16)
    for _ in range(2):
        tp, p2 = mm([t, p], p)
        t = t + tp
        p = p2
    t = t + mm([t], p)[0]
    o16 = jnp.where(same32 & jnp.logical_not(same16), a, 0.0)
    t = t + mm([t], mm([o16], t)[0])[0]
    o32 = jnp.where(same32, 0.0, a)
    return t + mm([t], mm([o32], t)[0])[0]


def _rwkv_kernel(z_ref, mu_ref, pv_ref, wl_ref, y_ref, prev_ref, st_ref, *, nb, nc, passes):
    c = RWKV_CHUNK
    kd = RWKV_DIM
    g = nb * nc
    rows = nc * c
    p_score, p_inv, p_apply, p_state = passes

    @pl.when(pl.program_id(0) == 0)
    def _():
        prev_ref[...] = jnp.zeros_like(prev_ref)
        st_ref[...] = jnp.zeros_like(st_ref)

    mu = mu_ref[...]
    row0 = lax.broadcasted_iota(jnp.int32, (rows, RWKV_IN), 0) == 0
    zs = []
    for b in range(nb):
        zb = z_ref[b]
        prev = jnp.where(row0, prev_ref[b], pltpu.roll(zb, 1, 0))
        prev_ref[b] = zb[rows - 1:rows, :]
        zs.append(zb + (prev - zb) * mu)
    zf = jnp.stack(zs, axis=0).reshape(g * c, RWKV_IN)

    pv = pv_ref[...]
    w0, a0, k_k, k_a, r_k, ln_w, ln_b = (pv[i:i + 1] for i in range(7))
    r = zf[:, 0:kd]
    k = zf[:, kd:2 * kd]
    v = zf[:, 2 * kd:3 * kd]
    lo = zf[:, 3 * kd:]
    lane = lax.broadcasted_iota(jnp.int32, (1, kd), 1)
    lora_in = jnp.where(lane < 64, jnp.tanh(lo), jnp.where(lane < 128, lo, _sigmoid(lo)))
    lin = _dg(lora_in.astype(BF16), wl_ref[...], NN)
    lw = -jnp.exp(-_softplus(-(w0 + lin[:, 0:kd])) - 0.5)
    a = _sigmoid(a0 + lin[:, kd:2 * kd])
    gate = lin[:, 2 * kd:]

    hrow = lax.broadcasted_iota(jnp.int32, (kd, kd), 0) // HEAD_DIM
    hcol = lax.broadcasted_iota(jnp.int32, (kd, kd), 1) // HEAD_DIM
    same_head = hrow == hcol
    ones_bd = same_head.astype(BF16)
    head_sum = lambda t: _mm_exact_rhs(t, ones_bd, NN)

    kkr = k * k_k
    kk = kkr / jnp.maximum(jnp.sqrt(head_sum(kkr * kkr)), 1e-12)
    kp = k * (1.0 + (a - 1.0) * k_a)
    bonus = head_sum(r * kp * r_k) * v

    to3 = lambda t: t.reshape(g, c, kd)
    r3, kp3, v3, kk3, a3, lw3 = (to3(t) for t in (r, kp, v, kk, a, lw))

    lower = (lax.broadcasted_iota(jnp.int32, (c, c), 0) >= lax.broadcasted_iota(jnp.int32, (c, c), 1))
    tri = jnp.broadcast_to(lower.astype(BF16)[None], (g, c, c))
    cum = _mm_exact_lhs(tri, lw3, BNN)
    cum_end = cum[:, c - 1:c, :]
    e_in = jnp.exp(cum)
    e_out = jnp.exp(-cum)
    e_tail = jnp.exp(cum_end - cum)
    p_end = jnp.exp(cum_end)
    beta = kk3 * a3
    ab = -kk3 * jnp.exp(cum - lw3)
    rb = r3 * e_in
    bb = beta * e_out
    kb = kp3 * e_out
    bt = beta * e_tail
    kt = kp3 * e_tail

    ri = lax.broadcasted_iota(jnp.int32, (c, kd), 0)
    cj = lax.broadcasted_iota(jnp.int32, (c, kd), 1) % c
    strict = ri > cj
    incl = ri >= cj
    sab, srb = _mm_heads([ab, rb], bb, ones_bd, p_score, BNT)
    sak, srk = _mm_heads([ab, rb], kb, ones_bd, p_score, BNT)
    a_ab = jnp.where(strict, sab, 0.0)
    a_rb = jnp.where(incl, srb, 0.0)
    a_ak = jnp.where(strict, sak, 0.0)
    a_rk = jnp.where(incl, srk, 0.0)
    t_inv = _unit_lower_inverse(a_ab, ones_bd, p_inv)

    app = lambda us, w: _mm_heads(us, w, ones_bd, p_apply, BNN)
    akv, arkv = app([a_ak, a_rk], v3)
    (w_all,) = app([t_inv], ab)
    (u0_all,) = app([t_inv], akv)
    rq = rb + app([a_rb], w_all)[0]
    yc = app([a_rb], u0_all)[0] + arkv

    bt_t = jnp.swapaxes(bt, 1, 2)
    kt_t = jnp.swapaxes(kt, 1, 2)
    eye = (lax.broadcasted_iota(jnp.int32, (kd, kd), 0)
           == lax.broadcasted_iota(jnp.int32, (kd, kd), 1))
    full = lambda u, w: _mm(u, w, BNN, p_apply)
    trans = jnp.where(same_head, full(bt_t, w_all), 0.0) + jnp.where(eye, p_end, 0.0)
    inject = jnp.where(same_head, full(bt_t, u0_all) + full(kt_t, v3), 0.0)

    per_chunk = lambda t, j: t.reshape((nb, nc) + t.shape[1:])[:, j]
    st = st_ref[...]
    ys = []
    for j in range(nc):
        y_st, st_new = _mm_shared([per_chunk(rq, j), per_chunk(trans, j)], st, BNN, p_state)
        ys.append(y_st + per_chunk(yc, j))
        st = st_new + per_chunk(inject, j)
    st_ref[...] = st
    y = jnp.stack(ys, axis=1)

    y2 = y.reshape(g * c, kd)
    mean = head_sum(y2) * (1.0 / HEAD_DIM)
    dv = y2 - mean
    var = head_sum(dv * dv) * (1.0 / HEAD_DIM)
    out = (dv * lax.rsqrt(var + RWKV_GN_EPS) * ln_w + ln_b + bonus) * gate
    y_ref[...] = out.reshape(nb, rows, kd).astype(y_ref.dtype)


def rwkv_mixer(z3, mu, pvec, w_lora, *, passes=(3, 3, 3, 3), chunks_per_step=4):
    b, s, _ = z3.shape
    c = RWKV_CHUNK * chunks_per_step
    return pl.pallas_call(
        functools.partial(_rwkv_kernel, nb=b, nc=chunks_per_step, passes=passes),
        grid=(s // c,),
        in_specs=[
            pl.BlockSpec((b, c, RWKV_IN), lambda t: (0, t, 0)),
            pl.BlockSpec((1, RWKV_IN), lambda t: (0, 0)),
            pl.BlockSpec((8, RWKV_DIM), lambda t: (0, 0)),
            pl.BlockSpec((RWKV_LORA_IN, 3 * RWKV_DIM), lambda t: (0, 0)),
        ],
        out_specs=pl.BlockSpec((b, c, RWKV_DIM), lambda t: (0, t, 0)),
        out_shape=jax.ShapeDtypeStruct((b, s, RWKV_DIM), BF16),
        scratch_shapes=[
            pltpu.VMEM((b, 1, RWKV_IN), F32),
            pltpu.VMEM((b, RWKV_DIM, RWKV_DIM), F32),
        ],
        compiler_params=_params("arbitrary"),
        name="rwkv_mixer",
    )(z3, mu, pvec, w_lora)


def _pool_kernel(z_ref, w_ref, sc_ref, y_ref, ext_ref, *, tp):
    t = pl.program_id(1)
    hl = POOL_HALO
    levels = len(POOL_WINDOWS)

    @pl.when(t == 0)
    def _():
        ext_ref[:, 0:hl, :] = jnp.zeros((levels, hl, POOL_DIM), F32)

    z = z_ref[0]
    sums = []
    cur = z
    for lvl, w in enumerate(POOL_WINDOWS):
        ext_ref[lvl, hl:hl + tp, :] = cur
        back = w // 2
        cur = cur + ext_ref[lvl, hl - back:hl - back + tp, :]
        sums.append(cur)
    for lvl in range(levels):
        ext_ref[lvl, 0:hl, :] = ext_ref[lvl, tp:tp + hl, :]

    lane = lax.broadcasted_iota(jnp.int32, (tp, POOL_DIM), 1)
    pos = (t * tp + lax.broadcasted_iota(jnp.int32, (tp, POOL_DIM), 0) + 1).astype(F32)
    grp = POOL_DIM // levels
    win_sum = sums[-1]
    count = jnp.minimum(pos, float(POOL_WINDOWS[-1]))
    for lvl in range(levels - 2, -1, -1):
        in_grp = lane < (lvl + 1) * grp
        win_sum = jnp.where(in_grp, sums[lvl], win_sum)
        count = jnp.where(in_grp, jnp.minimum(pos, float(POOL_WINDOWS[lvl])), count)
    pooled = win_sum / count - z
    y = _dg(pooled.astype(BF16), w_ref[...], NN) * sc_ref[...]
    y_ref[0] = y.astype(y_ref.dtype)


def pool_mixer(zp3, w_bd, scale, *, tp):
    b, s, _ = zp3.shape
    return pl.pallas_call(
        functools.partial(_pool_kernel, tp=tp),
        grid=(b, s // tp),
        in_specs=[
            pl.BlockSpec((1, tp, POOL_DIM), lambda i, t: (i, t, 0)),
            pl.BlockSpec((POOL_DIM, POOL_DIM), lambda i, t: (0, 0)),
            pl.BlockSpec((1, POOL_DIM), lambda i, t: (0, 0)),
        ],
        out_specs=pl.BlockSpec((1, tp, POOL_DIM), lambda i, t: (i, t, 0)),
        out_shape=jax.ShapeDtypeStruct((b, s, POOL_DIM), BF16),
        scratch_shapes=[pltpu.VMEM((len(POOL_WINDOWS), POOL_HALO + tp, POOL_DIM), F32)],
        compiler_params=_params("arbitrary", "arbitrary"),
        name="pool_mixer",
    )(zp3, w_bd, scale)


def _fox_kernel(q_ref, k_ref, vt_ref, cc_ref, o_ref, ka_ref, m_ref, l_ref, acc_ref, sa_ref, sb_ref,
                *, tq, nblk):
    hp = pl.program_id(1)
    i = pl.program_id(2)
    lane = lax.broadcasted_iota(jnp.int32, (1, LANES), 1)
    spare = [HEAD_DIM * (1 - hh) for hh in range(2)]

    @pl.when(i == 0)
    def _():
        def fill(jb, carry):
            start = pl.multiple_of(jb * tq, tq)
            blk = cc_ref[0, pl.ds(start, tq), :]
            k2 = k_ref[0, pl.ds(start, tq), :]
            for hh in range(2):
                col = jnp.sum(jnp.where(lane == 2 * hp + hh, blk, 0.0), axis=-1, keepdims=True)
                hi, mid, lo = _split3(col * (-LOG2_E))
                ka = k2
                for off, part in enumerate((hi, mid, lo)):
                    ka = jnp.where(lane == spare[hh] + off, part, ka)
                ka_ref[hh, pl.ds(start, tq), :] = ka
            return carry

        lax.fori_loop(0, nblk, fill, 0)

    q2 = q_ref[0]
    m_ref[...] = jnp.full(m_ref.shape, NEG_BIG, F32)
    l_ref[...] = jnp.zeros(l_ref.shape, F32)
    acc_ref[...] = jnp.zeros(acc_ref.shape, F32)
    units = [(lane >= spare[hh]) & (lane < spare[hh] + 3) for hh in range(2)]
    qh = [jnp.where(lane // HEAD_DIM == hh, q2, units[hh].astype(q2.dtype)) for hh in range(2)]

    def scores_into(j, dst):
        start = pl.multiple_of(j * tq, tq)
        for hh in range(2):
            dst[hh] = _dg(ka_ref[hh, pl.ds(start, tq), :], qh[hh], NT)

    def consume(j, src, masked):
        for hh in range(2):
            s = src[hh]
            if masked:
                kpos = lax.broadcasted_iota(jnp.int32, (tq, tq), 0)
                qpos = lax.broadcasted_iota(jnp.int32, (tq, tq), 1)
                s = jnp.where(kpos <= qpos, s, NEG_BIG)
            m_old = m_ref[hh][0:1, :]
            m_new = jnp.maximum(m_old, jnp.max(s, axis=0, keepdims=True))
            alpha = jnp.exp2(m_old - m_new)
            p = jnp.exp2(s - m_new)
            l_new = alpha * l_ref[hh][0:1, :] + jnp.sum(p, axis=0, keepdims=True)
            vth = vt_ref[0, j, hh * HEAD_DIM:(hh + 1) * HEAD_DIM, :]
            acc_ref[hh] = alpha * acc_ref[hh] + _dg(vth, p.astype(BF16), NN)
            l_ref[hh] = jnp.broadcast_to(l_new, (8, tq))
            m_ref[hh] = jnp.broadcast_to(m_new, (8, tq))

    scores_into(0, sa_ref)

    def pair(jj, carry):
        j = 2 * jj
        scores_into(j + 1, sb_ref)
        consume(j, sa_ref, False)
        scores_into(j + 2, sa_ref)
        consume(j + 1, sb_ref, False)
        return carry

    lax.fori_loop(0, i // 2, pair, 0)

    @pl.when(i % 2 == 0)
    def _():
        consume(i, sa_ref, True)

    @pl.when(i % 2 == 1)
    def _():
        scores_into(i, sb_ref)
        consume(i - 1, sa_ref, False)
        consume(i, sb_ref, True)

    ot = jnp.concatenate([acc_ref[hh] / l_ref[hh][0:1, :] for hh in range(2)], axis=0)
    o_ref[0] = ot.T.astype(o_ref.dtype)


def fox_attention(q3, k3, vt4, cumc3, *, tq):
    b, s, _ = q3.shape
    nq = s // tq
    pairs = FOX_HEADS // 2
    return pl.pallas_call(
        functools.partial(_fox_kernel, tq=tq, nblk=nq),
        grid=(b, pairs, nq),
        in_specs=[
            pl.BlockSpec((1, tq, LANES), lambda bi, hp, i: (bi, i, hp)),
            pl.BlockSpec((1, s, LANES), lambda bi, hp, i: (bi, 0, hp)),
            pl.BlockSpec((1, nq, LANES, tq), lambda bi, hp, i: (bi, 0, hp, 0)),
            pl.BlockSpec((1, s, LANES), lambda bi, hp, i: (bi, 0, 0)),
        ],
        out_specs=pl.BlockSpec((1, tq, LANES), lambda bi, hp, i: (bi, i, hp)),
        out_shape=jax.ShapeDtypeStruct((b, s, FOX_DIM), BF16),
        scratch_shapes=[
            pltpu.VMEM((2, s, LANES), BF16),
            pltpu.VMEM((2, 8, tq), F32),
            pltpu.VMEM((2, 8, tq), F32),
            pltpu.VMEM((2, HEAD_DIM, tq), F32),
            pltpu.VMEM((2, tq, tq), F32),
            pltpu.VMEM((2, tq, tq), F32),
        ],
        compiler_params=_params("arbitrary", "arbitrary", "arbitrary"),
        name="fox_attention",
    )(q3, k3, vt4, cumc3)


def _outproj_kernel(yr_ref, yp_ref, yf_ref, x_ref, mod_ref, wo_ref, n2_ref, rw_ref, rb_ref,
                    x1_ref, h2_ref, ti_ref, tg_ref):
    mod = mod_ref[0]
    mixed = (_dg(yr_ref[...], wo_ref[0:RWKV_DIM, :], NN)
             + _dg(yp_ref[...], wo_ref[RWKV_DIM:RWKV_DIM + POOL_DIM, :], NN)
             + _dg(yf_ref[...], wo_ref[RWKV_DIM + POOL_DIM:, :], NN))
    x1 = x_ref[...] + mod[2:3] * mixed
    x1_ref[...] = x1
    h2 = _rms_modulate(x1, n2_ref[...], mod[4:5], mod[3:4])
    h2_ref[...] = h2

    tm = x1.shape[0]
    lane = lax.broadcasted_iota(jnp.int32, (tm, LANES), 1)
    lane_f = lane.astype(F32)
    logits = _mm(h2, rw_ref[...], NN, passes=3) + rb_ref[...]
    work = jnp.where(lane < N_EXPERTS, logits, NEG_BIG)
    top_i = jnp.zeros((tm, LANES), F32)
    top_e = jnp.zeros((tm, LANES), F32)
    v0 = None
    for kk in range(TOP_K):
        vmax = jnp.max(work, axis=-1, keepdims=True)
        idx = jnp.min(jnp.where(work == vmax, lane_f, float(LANES)), axis=-1, keepdims=True)
        if kk == 0:
            v0 = vmax
        top_i = jnp.where(lane == kk, idx, top_i)
        top_e = jnp.where(lane == kk, jnp.exp(vmax - v0), top_e)
        work = jnp.where(lane_f == idx, NEG_BIG, work)
    ti_ref[...] = top_i.astype(jnp.int32)
    tg_ref[...] = top_e / jnp.sum(top_e, axis=-1, keepdims=True)


def out_projection(yr, yp, yf, x2, mod, w_out, norm2, rw, rb, *, seq, tm):
    n, d = x2.shape
    tpb = seq // tm
    row = lambda i: (i, 0)
    const = lambda i: (0, 0)
    return pl.pallas_call(
        _outproj_kernel,
        grid=(n // tm,),
        in_specs=[
            pl.BlockSpec((tm, RWKV_DIM), row),
            pl.BlockSpec((tm, POOL_DIM), row),
            pl.BlockSpec((tm, FOX_DIM), row),
            pl.BlockSpec((tm, d), row),
            pl.BlockSpec((1, 6, d), lambda i: (i // tpb, 0, 0)),
            pl.BlockSpec((d, d), const),
            pl.BlockSpec((1, d), const),
            pl.BlockSpec((d, LANES), const),
            pl.BlockSpec((1, LANES), const),
        ],
        out_specs=(
            pl.BlockSpec((tm, d), row),
            pl.BlockSpec((tm, d), row),
            pl.BlockSpec((tm, LANES), row),
            pl.BlockSpec((tm, LANES), row),
        ),
        out_shape=(
            jax.ShapeDtypeStruct((n, d), F32),
            jax.ShapeDtypeStruct((n, d), F32),
            jax.ShapeDtypeStruct((n, LANES), jnp.int32),
            jax.ShapeDtypeStruct((n, LANES), F32),
        ),
        compiler_params=_params("arbitrary"),
        name="out_projection",
    )(yr, yp, yf, x2, mod, w_out, norm2, rw, rb)


def _rank_kernel(ti_ref, rank_ref, cnt_ref, carry_ref, *, tr):
    @pl.when(pl.program_id(0) == 0)
    def _():
        carry_ref[...] = jnp.zeros_like(carry_ref)

    ti = ti_ref[...]
    lane = lax.broadcasted_iota(jnp.int32, (tr, LANES), 1)
    hits = [ti[:, kk:kk + 1] == lane for kk in range(TOP_K)]
    cnt = hits[0].astype(F32)
    for kk in range(1, TOP_K):
        cnt = cnt + hits[kk].astype(F32)
    before = (lax.broadcasted_iota(jnp.int32, (tr, tr), 0)
              > lax.broadcasted_iota(jnp.int32, (tr, tr), 1)).astype(BF16)
    pre = _dg(before, cnt.astype(BF16), NN) + carry_ref[...]
    rank = jnp.zeros((tr, LANES), F32)
    for kk in range(TOP_K):
        rk = jnp.sum(jnp.where(hits[kk], pre, 0.0), axis=-1, keepdims=True)
        rank = jnp.where(lane == kk, rk, rank)
    rank_ref[...] = rank.astype(jnp.int32)
    total = carry_ref[...] + jnp.sum(cnt, axis=0, keepdims=True)
    carry_ref[...] = total
    cnt_ref[...] = jnp.broadcast_to(total, cnt_ref.shape)


def moe_rank(top_i, *, tr):
    n = top_i.shape[0]
    return pl.pallas_call(
        functools.partial(_rank_kernel, tr=tr),
        grid=(n // tr,),
        in_specs=[pl.BlockSpec((tr, LANES), lambda i: (i, 0))],
        out_specs=(pl.BlockSpec((tr, LANES), lambda i: (i, 0)),
                   pl.BlockSpec((8, LANES), lambda i: (0, 0))),
        out_shape=(jax.ShapeDtypeStruct((n, LANES), jnp.int32),
                   jax.ShapeDtypeStruct((8, LANES), F32)),
        scratch_shapes=[pltpu.VMEM((1, LANES), F32)],
        compiler_params=_params("arbitrary"),
        name="moe_rank",
    )(top_i)


def _routing_tables(top_i, *, rows_per_block, tr):
    n = top_i.shape[0]
    r = rows_per_block
    rank, cnt = moe_rank(top_i, tr=tr)
    counts = cnt[0, :N_EXPERTS].astype(jnp.int32)
    padded = (counts + r - 1) // r * r
    pend = jnp.cumsum(padded)
    pstart = pend - padded
    e_sel = top_i[:, :TOP_K]
    onehot = e_sel[:, :, None] == jnp.arange(N_EXPERTS, dtype=jnp.int32)[None, None, :]
    dest = (jnp.sum(jnp.where(onehot, pstart[None, None, :], 0), axis=-1) + rank[:, :TOP_K]).astype(jnp.int32)
    nb = -(-(n * TOP_K + N_EXPERTS * (r - 1)) // r)
    bstart = jnp.arange(nb, dtype=jnp.int32) * r
    block_e = jnp.minimum(jnp.sum(bstart[:, None] >= pend[None, :], axis=1), N_EXPERTS - 1).astype(jnp.int32)
    in_e = block_e[:, None] == jnp.arange(N_EXPERTS, dtype=jnp.int32)[None, :]
    seg_end = jnp.sum(jnp.where(in_e, (pstart + counts)[None, :], 0), axis=1)
    nvalid = jnp.clip(seg_end - bstart, 0, r).astype(jnp.int32)
    nused = (pend[-1] // r).astype(jnp.int32).reshape(1)
    pad = jnp.stack([pstart + counts, padded - counts]).astype(jnp.int32)
    return dest, block_e, nvalid, nused, pad, nb


def _zero_fill(pad_ref, nu_ref, xs_ref, zero_ref, zsem, *, r, nb):
    row_sem, block_sem = zsem.at[0], zsem.at[1]
    zero_ref[...] = jnp.zeros_like(zero_ref)

    def each_copy(fn):
        def per_expert(e, carry):
            start = pad_ref[0, e]

            def per_row(i, c2):
                fn(pltpu.make_async_copy(zero_ref.at[pl.ds(0, 1)], xs_ref.at[pl.ds(start + i, 1)], row_sem))
                return c2

            lax.fori_loop(0, pad_ref[1, e], per_row, 0)
            return carry

        lax.fori_loop(0, N_EXPERTS, per_expert, 0)

        def per_block(b, carry):
            fn(pltpu.make_async_copy(zero_ref, xs_ref.at[pl.ds(pl.multiple_of(b * r, r), r)], block_sem))
            return carry

        lax.fori_loop(nu_ref[0], nb, per_block, 0)

    each_copy(lambda cp: cp.start())
    each_copy(lambda cp: cp.wait())


def _dispatch_kernel(dest_ref, pad_ref, nu_ref, h_ref, xs_ref, zero_ref, sem, zsem, *, td, r, nb):
    @pl.when(pl.program_id(0) == 0)
    def _():
        _zero_fill(pad_ref, nu_ref, xs_ref, zero_ref, zsem, r=r, nb=nb)

    for t in range(td):
        for kk in range(TOP_K):
            pltpu.make_async_copy(h_ref.at[pl.ds(t, 1)],
                                  xs_ref.at[pl.ds(dest_ref[0, 0, TOP_K * t + kk], 1)], sem
                                  ).start(priority=kk % 2)
    for kk in range(TOP_K):
        pltpu.make_async_copy(h_ref, xs_ref.at[pl.ds(0, td)], sem).wait()


def moe_dispatch(h2, dest2, pad, nused, *, nb, r, td):
    n, d = h2.shape
    return pl.pallas_call(
        functools.partial(_dispatch_kernel, td=td, r=r, nb=nb),
        grid=(n // td,),
        in_specs=[
            pl.BlockSpec((1, 1, TOP_K * td), lambda i: (i, 0, 0), memory_space=pltpu.SMEM),
            pl.BlockSpec(memory_space=pltpu.SMEM),
            pl.BlockSpec(memory_space=pltpu.SMEM),
            pl.BlockSpec((td, d), lambda i: (i, 0)),
        ],
        out_specs=pl.BlockSpec(memory_space=pl.ANY),
        out_shape=jax.ShapeDtypeStruct((nb * r, d), F32),
        scratch_shapes=[pltpu.VMEM((r, d), F32), pltpu.SemaphoreType.DMA, pltpu.SemaphoreType.DMA((2,))],
        compiler_params=_params("arbitrary"),
        name="moe_dispatch",
    )(dest2, pad, nused, h2)


def _expert_kernel(be_ref, nv_ref, nu_ref, x_ref, wgu_ref, bgu_ref, wd_ref, bd_ref, y_ref,
                   wgu_bf, wd_bf, cur_ref):
    b = pl.program_id(0)
    e = be_ref[b]
    nv = nv_ref[b]

    @pl.when(b == 0)
    def _():
        cur_ref[0] = -1

    @pl.when((nv > 0) & (cur_ref[0] != e))
    def _():
        cur_ref[0] = e
        step = 128

        def cast(i, carry):
            s = pl.multiple_of(i * step, step)
            wgu_bf[pl.ds(s, step), :] = wgu_ref[0, 0, pl.ds(s, step), :].astype(BF16)
            wd_bf[pl.ds(s, step), :] = wd_ref[0, 0, pl.ds(s, step), :].astype(BF16)
            return carry

        lax.fori_loop(0, D_MODEL // step, cast, 0)

    @pl.when(nv == 0)
    def _():
        y_ref[...] = jnp.zeros_like(y_ref)

    @pl.when(nv > 0)
    def _():
        x = x_ref[...].astype(BF16)
        gu = _dg(x, wgu_bf[...], NN) + bgu_ref[0, 0]
        gate = jnp.minimum(gu[:, :D_FF], SWIGLU_LIMIT)
        up = jnp.clip(gu[:, D_FF:], -SWIGLU_LIMIT, SWIGLU_LIMIT)
        act = (up + 1.0) * (gate * _sigmoid(SWIGLU_ALPHA * gate))
        y_ref[...] = _dg(act.astype(BF16), wd_bf[...], NN) + bd_ref[0, 0]


def moe_experts(xs, block_e, nvalid, nused, w_gu, b_gu, w_down, b_down, *, layer, r):
    rows, d = xs.shape
    nb = rows // r
    blk = lambda b, be, nv, nu: (jnp.minimum(b, nu[0] - 1), 0)
    wsel = lambda b, be, nv, nu: (layer, be[b], 0, 0)
    grid_spec = pltpu.PrefetchScalarGridSpec(
        num_scalar_prefetch=3,
        grid=(nb,),
        in_specs=[
            pl.BlockSpec((r, d), blk),
            pl.BlockSpec((1, 1, d, 2 * D_FF), wsel),
            pl.BlockSpec((1, 1, 1, 2 * D_FF), wsel),
            pl.BlockSpec((1, 1, D_FF, d), wsel),
            pl.BlockSpec((1, 1, 1, d), wsel),
        ],
        out_specs=pl.BlockSpec((r, d), lambda b, be, nv, nu: (b, 0)),
        scratch_shapes=[
            pltpu.VMEM((d, 2 * D_FF), BF16),
            pltpu.VMEM((D_FF, d), BF16),
            pltpu.SMEM((1,), jnp.int32),
        ],
    )
    return pl.pallas_call(
        _expert_kernel,
        grid_spec=grid_spec,
        out_shape=jax.ShapeDtypeStruct((rows, d), F32),
        compiler_params=_params("arbitrary"),
        name="moe_experts",
    )(block_e, nvalid, nused, xs, w_gu, b_gu, w_down, b_down)


def _combine_kernel(dest0_ref, destn_ref, yb_ref, tg_ref, x_ref, mod_ref, nf_ref, o_ref, buf0, buf1, sem,
                    *, tc, nt, final_norm):
    i = pl.program_id(0)
    bufs = (buf0, buf1)

    def gather(tab_ref, half):
        for t in range(tc):
            for kk in range(TOP_K):
                pltpu.make_async_copy(yb_ref.at[pl.ds(tab_ref[0, 0, TOP_K * t + kk], 1)],
                                      bufs[half].at[kk, pl.ds(t, 1)], sem.at[half]).start(priority=kk % 2)

    def gather_wait(half):
        for kk in range(TOP_K):
            pltpu.make_async_copy(yb_ref.at[pl.ds(0, tc)], bufs[half].at[kk], sem.at[half]).wait()

    @pl.when(i == 0)
    def _():
        gather(dest0_ref, 0)

    def step(cur):
        @pl.when(i + 1 < nt)
        def _():
            gather(destn_ref, 1 - cur)

        gather_wait(cur)
        _combine_tile(bufs[cur], tg_ref, x_ref, mod_ref, nf_ref, o_ref, final_norm=final_norm)

    for parity in range(2):
        pl.when(i % 2 == parity)(functools.partial(step, parity))


def _combine_tile(buf, tg_ref, x_ref, mod_ref, nf_ref, o_ref, *, final_norm):
    tg = tg_ref[...]
    y = tg[:, 0:1] * buf[0]
    for kk in range(1, TOP_K):
        y = y + tg[:, kk:kk + 1] * buf[kk]
    x = x_ref[...] + mod_ref[0][5:6] * y
    if final_norm:
        ms = jnp.mean(x * x, axis=-1, keepdims=True)
        x = x * lax.rsqrt(ms + NORM_EPS) * nf_ref[...]
    o_ref[...] = x


def moe_combine(yb, dest2, tg, x1, mod, norm_f, *, seq, tc, final_norm):
    n, d = x1.shape
    tpb = seq // tc
    nt = n // tc
    row = lambda i: (i, 0)
    return pl.pallas_call(
        functools.partial(_combine_kernel, tc=tc, nt=nt, final_norm=final_norm),
        grid=(nt,),
        in_specs=[
            pl.BlockSpec((1, 1, TOP_K * tc), lambda i: (0, 0, 0), memory_space=pltpu.SMEM),
            pl.BlockSpec((1, 1, TOP_K * tc), lambda i: (jnp.minimum(i + 1, nt - 1), 0, 0),
                         memory_space=pltpu.SMEM),
            pl.BlockSpec(memory_space=pl.ANY),
            pl.BlockSpec((tc, LANES), row),
            pl.BlockSpec((tc, d), row),
            pl.BlockSpec((1, 6, d), lambda i: (i // tpb, 0, 0)),
            pl.BlockSpec((1, d), lambda i: (0, 0)),
        ],
        out_specs=pl.BlockSpec((tc, d), row),
        out_shape=jax.ShapeDtypeStruct((n, d), F32),
        scratch_shapes=[pltpu.VMEM((TOP_K, tc, d), F32), pltpu.VMEM((TOP_K, tc, d), F32),
                        pltpu.SemaphoreType.DMA((2,))],
        compiler_params=_params("arbitrary"),
        name="moe_combine",
    )(dest2, dest2, yb, tg, x1, mod, norm_f)


def _block_diag(blocks):
    g, m, _ = blocks.shape
    out = jnp.zeros((g * m, g * m), blocks.dtype)
    for i in range(g):
        out = out.at[i * m:(i + 1) * m, i * m:(i + 1) * m].set(blocks[i])
    return out


def _pad_cols(w, cols):
    return jnp.zeros(w.shape[:-1] + (cols,), w.dtype).at[..., :w.shape[-1]].set(w)


def trunk(x, c, ada_w, ada_b, norm1, w_in, w_out, rwkv_mu, rwkv_w0, rwkv_w_up, rwkv_a0, rwkv_a_up,
          rwkv_g_up, rwkv_k_k, rwkv_k_a, rwkv_r_k, rwkv_ln_w, rwkv_ln_b, pool_w, pool_scale, fox_b_f,
          norm2, router_w, router_b, moe_w_gu, moe_b_gu, moe_w_down, moe_b_down, norm_f,
          *, tm=512, tp=512, tq=512, expert_rows=256, td=256, tc=256, rwkv_passes=(1, 3, 1, 3)):
    batch, seq, d = x.shape
    depth = ada_w.shape[0]
    n = batch * seq
    mod_all = ada_modulation(c, ada_w, ada_b)
    x2 = x.reshape(n, d)
    nf = norm_f.reshape(1, d)
    b_gu4 = moe_b_gu.reshape(depth, N_EXPERTS, 1, 2 * D_FF)
    b_down4 = moe_b_down.reshape(depth, N_EXPERTS, 1, d)
    for l in range(depth):
        mod = mod_all[l]
        w_main = w_in[l][:, :MAIN_COLS].astype(BF16)
        w_f = _pad_cols(w_in[l][:, MAIN_COLS:], LANES)
        b_f = _pad_cols(fox_b_f[l].reshape(1, FOX_HEADS), LANES)
        w_lora = jnp.zeros((RWKV_LORA_IN, 3 * RWKV_DIM), F32)
        w_lora = w_lora.at[0:64, 0:RWKV_DIM].set(rwkv_w_up[l])
        w_lora = w_lora.at[64:128, RWKV_DIM:2 * RWKV_DIM].set(rwkv_a_up[l])
        w_lora = w_lora.at[128:256, 2 * RWKV_DIM:].set(rwkv_g_up[l]).astype(BF16)
        pvec = jnp.stack([rwkv_w0[l], rwkv_a0[l], rwkv_k_k[l], rwkv_k_a[l], rwkv_r_k[l].reshape(-1),
                          rwkv_ln_w[l], rwkv_ln_b[l], jnp.zeros((RWKV_DIM,), F32)], axis=0)
        w_pool = _block_diag(pool_w[l]).astype(BF16)
        rw = _pad_cols(router_w[l], LANES)
        rb = _pad_cols(router_b[l].reshape(1, N_EXPERTS), LANES)

        zr, zp, q, k, vt, cumc = in_projection(
            x2, mod, norm1[l].reshape(1, d), w_main, w_f, b_f, batch=batch, seq=seq, tm=tm, tv=tq)
        yr = rwkv_mixer(zr.reshape(batch, seq, RWKV_IN), rwkv_mu[l].reshape(1, RWKV_IN), pvec, w_lora,
                        passes=rwkv_passes)
        yp = pool_mixer(zp.reshape(batch, seq, POOL_DIM), w_pool, pool_scale[l].reshape(1, POOL_DIM), tp=tp)
        to3 = lambda t: t.reshape(batch, seq, t.shape[-1])
        yf = fox_attention(to3(q), to3(k), vt, to3(cumc), tq=tq)
        x1, h2, top_i, top_g = out_projection(
            yr.reshape(n, RWKV_DIM), yp.reshape(n, POOL_DIM), yf.reshape(n, FOX_DIM), x2, mod,
            w_out[l].astype(BF16), norm2[l].reshape(1, d), rw, rb, seq=seq, tm=tm)

        dest, block_e, nvalid, nused, pad, nb = _routing_tables(top_i, rows_per_block=expert_rows, tr=tm)
        xs = moe_dispatch(h2, dest.reshape(n // td, 1, TOP_K * td), pad, nused, nb=nb, r=expert_rows, td=td)
        yb = moe_experts(xs, block_e, nvalid, nused, moe_w_gu, b_gu4, moe_w_down, b_down4,
                         layer=l, r=expert_rows)
        x2 = moe_combine(yb, dest.reshape(n // tc, 1, TOP_K * tc), top_g, x1, mod, nf, seq=seq, tc=tc,
                         final_norm=(l == depth - 1))
    return x2.reshape(batch, seq, d)


def kernel(x, c, ada_w, ada_b, norm1, w_in, w_out, rwkv_mu, rwkv_w0, rwkv_w_up, rwkv_a0, rwkv_a_up, rwkv_g_up, rwkv_k_k, rwkv_k_a, rwkv_r_k, rwkv_ln_w, rwkv_ln_b, pool_w, pool_scale, fox_b_f, norm2, router_w, router_b, moe_w_gu, moe_b_gu, moe_w_down, moe_b_down, norm_f):
    return trunk(x, c, ada_w, ada_b, norm1, w_in, w_out, rwkv_mu, rwkv_w0, rwkv_w_up, rwkv_a0, rwkv_a_up,
                 rwkv_g_up, rwkv_k_k, rwkv_k_a, rwkv_r_k, rwkv_ln_w, rwkv_ln_b, pool_w, pool_scale, fox_b_f,
                 norm2, router_w, router_b, moe_w_gu, moe_b_gu, moe_w_down, moe_b_down, norm_f)
```

```python
import functools

import jax
import jax.numpy as jnp
from jax import lax
from jax.experimental import pallas as pl
from jax.experimental.pallas import tpu as pltpu

F32 = jnp.float32
BF16 = jnp.bfloat16

D_MODEL = 1024
HEAD_DIM = 64
RWKV_DIM = 256
RWKV_HEADS = 4
RWKV_LORA_IN = 256
RWKV_GN_EPS = 64e-5
POOL_DIM = 256
POOL_WINDOWS = (2, 4, 8, 16)
POOL_HALO = 16
FOX_DIM = 512
FOX_HEADS = 8
N_EXPERTS = 32
TOP_K = 4
D_FF = 1024
SWIGLU_LIMIT = 7.0
SWIGLU_ALPHA = 1.702
NORM_EPS = 1e-6
RWKV_IN = 1024
MAIN_COLS = RWKV_IN + POOL_DIM + 3 * FOX_DIM
LANES = 128
RWKV_CHUNK = 64
NEG_BIG = -1e30
LOG2_E = 1.4426950408889634
VMEM_LIMIT = 48 * 1024 * 1024
NN = (((1,), (0,)), ((), ()))
NT = (((1,), (1,)), ((), ()))
BNN = (((2,), (1,)), ((0,), (0,)))
BNT = (((2,), (2,)), ((0,), (0,)))


def _dg(a, b, dims):
    return lax.dot_general(a, b, dims, preferred_element_type=F32)


def _split2(x):
    hi = x.astype(BF16)
    lo = (x - hi.astype(F32)).astype(BF16)
    return hi, lo


def _split3(x):
    hi = x.astype(BF16)
    r = x - hi.astype(F32)
    mid = r.astype(BF16)
    lo = (r - mid.astype(F32)).astype(BF16)
    return hi, mid, lo


def _mm(a, b, dims, passes=1):
    if passes == 1:
        return _dg(a.astype(BF16), b.astype(BF16), dims)
    ah, al = _split2(a)
    bh, bl = _split2(b)
    return _dg(ah, bh, dims) + (_dg(ah, bl, dims) + _dg(al, bh, dims))


def _mm_exact_lhs(a_bf16, b, dims):
    hi, mid, lo = _split3(b)
    return _dg(a_bf16, hi, dims) + (_dg(a_bf16, mid, dims) + _dg(a_bf16, lo, dims))


def _mm_exact_rhs(a, b_bf16, dims):
    hi, mid, lo = _split3(a)
    return _dg(hi, b_bf16, dims) + (_dg(mid, b_bf16, dims) + _dg(lo, b_bf16, dims))


def _sigmoid(x):
    return 1.0 / (1.0 + jnp.exp(-x))


def _softplus(x):
    return jnp.maximum(x, 0.0) + jnp.log1p(jnp.exp(-jnp.abs(x)))


def _params(*sem):
    return pltpu.CompilerParams(dimension_semantics=sem, vmem_limit_bytes=VMEM_LIMIT)


def _ada_kernel(c_ref, w_ref, b_ref, o_ref):
    c = c_ref[...]
    cond = c * _sigmoid(c)
    o_ref[0] = _mm(cond, w_ref[0], NN, passes=3) + b_ref[0]


def ada_modulation(c, ada_w, ada_b):
    L, D, D6 = ada_w.shape
    B = c.shape[0]
    rows = 8
    cp = jnp.zeros((rows, D), F32).at[:B].set(c)
    out = pl.pallas_call(
        _ada_kernel,
        grid=(L, D6 // D),
        in_specs=[
            pl.BlockSpec((rows, D), lambda l, j: (0, 0)),
            pl.BlockSpec((1, D, D), lambda l, j: (l, 0, j)),
            pl.BlockSpec((1, 1, D), lambda l, j: (l, 0, j)),
        ],
        out_specs=pl.BlockSpec((1, rows, D), lambda l, j: (l, 0, j)),
        out_shape=jax.ShapeDtypeStruct((L, rows, D6), F32),
        compiler_params=_params("arbitrary", "arbitrary"),
        name="ada_modulation",
    )(cp, ada_w, ada_b.reshape(L, 1, D6))
    return out[:, :B].reshape(L, B, D6 // D, D)


def _rms_modulate(x, gain, scale, shift):
    ms = jnp.mean(x * x, axis=-1, keepdims=True)
    return x * lax.rsqrt(ms + NORM_EPS) * gain * (1.0 + scale) + shift


def _inproj_kernel(x_ref, mod_ref, n1_ref, wm_ref, wf_ref, bf_ref,
                   zr_ref, zp_ref, q_ref, k_ref, vt_ref, cumc_ref, carry_ref,
                   *, tiles_per_batch, tm, tv):
    i = pl.program_id(0)

    @pl.when(i % tiles_per_batch == 0)
    def _():
        carry_ref[...] = jnp.zeros_like(carry_ref)

    mod = mod_ref[0]
    h = _rms_modulate(x_ref[...], n1_ref[...], mod[1:2], mod[0:1])
    z = _dg(h.astype(BF16), wm_ref[...], NN)
    zr_ref[...] = z[:, :RWKV_IN]
    zp_ref[...] = z[:, RWKV_IN:RWKV_IN + POOL_DIM]
    o = RWKV_IN + POOL_DIM
    q_ref[...] = (z[:, o:o + FOX_DIM] * (HEAD_DIM ** -0.5 * LOG2_E)).astype(BF16)
    k_ref[...] = z[:, o + FOX_DIM:o + 2 * FOX_DIM].astype(BF16)
    for u in range(tm // tv):
        vt_ref[0, u] = z[u * tv:(u + 1) * tv, o + 2 * FOX_DIM:o + 3 * FOX_DIM].T.astype(BF16)

    fl = _mm(h, wf_ref[...], NN, passes=3) + bf_ref[...]
    logf = jnp.minimum(fl, 0.0) - jnp.log1p(jnp.exp(-jnp.abs(fl)))
    row = lax.broadcasted_iota(jnp.int32, (tm, tm), 0)
    col = lax.broadcasted_iota(jnp.int32, (tm, tm), 1)
    tri = (row >= col).astype(BF16)
    cum = _mm_exact_lhs(tri, logf, NN) + carry_ref[...]
    carry_ref[...] = cum[tm - 1:tm, :]
    cumc_ref[...] = cum


def in_projection(x2, mod, norm1, w_main, w_f, b_f, *, batch, seq, tm, tv):
    n, d = x2.shape
    tpb = seq // tm
    row = lambda i: (i, 0)
    const = lambda i: (0, 0)
    out_shapes = (
        jax.ShapeDtypeStruct((n, RWKV_IN), F32),
        jax.ShapeDtypeStruct((n, POOL_DIM), F32),
        jax.ShapeDtypeStruct((n, FOX_DIM), BF16),
        jax.ShapeDtypeStruct((n, FOX_DIM), BF16),
        jax.ShapeDtypeStruct((batch, seq // tv, FOX_DIM, tv), BF16),
        jax.ShapeDtypeStruct((n, LANES), F32),
    )
    return pl.pallas_call(
        functools.partial(_inproj_kernel, tiles_per_batch=tpb, tm=tm, tv=tv),
        grid=(n // tm,),
        in_specs=[
            pl.BlockSpec((tm, d), row),
            pl.BlockSpec((1, 6, d), lambda i: (i // tpb, 0, 0)),
            pl.BlockSpec((1, d), const),
            pl.BlockSpec((d, MAIN_COLS), const),
            pl.BlockSpec((d, LANES), const),
            pl.BlockSpec((1, LANES), const),
        ],
        out_specs=(
            pl.BlockSpec((tm, RWKV_IN), row),
            pl.BlockSpec((tm, POOL_DIM), row),
            pl.BlockSpec((tm, FOX_DIM), row),
            pl.BlockSpec((tm, FOX_DIM), row),
            pl.BlockSpec((1, tm // tv, FOX_DIM, tv), lambda i: (i // tpb, i % tpb, 0, 0)),
            pl.BlockSpec((tm, LANES), row),
        ),
        out_shape=out_shapes,
        scratch_shapes=[pltpu.VMEM((1, LANES), F32)],
        compiler_params=_params("arbitrary"),
        name="in_projection",
    )(x2, mod, norm1, w_main, w_f, b_f)


def _head_blocks(y, same_head):
    return jnp.concatenate([y] * RWKV_HEADS, axis=1) * same_head


def _mm_shared(xs, y, dims, passes):
    sizes = [x.shape[1] for x in xs]
    x = xs[0] if len(xs) == 1 else jnp.concatenate(xs, axis=1)
    m = x.shape[1]
    if passes == 1:
        out = _dg(x.astype(BF16), y.astype(BF16), dims)
    else:
        xh, xl = _split2(x)
        yh, yl = y if isinstance(y, tuple) else _split2(y)
        both = _dg(jnp.concatenate([xh, xl], axis=1), yh, dims)
        out = both[:, :m] + (both[:, m:] + _dg(xh, yl, dims))
    outs, start = [], 0
    for s in sizes:
        outs.append(out[:, start:start + s])
        start += s
    return outs


def _mm_heads(xs, y, same_head, passes, dims):
    if passes == 1:
        return _mm_shared(xs, _head_blocks(y.astype(BF16), same_head), dims, 1)
    yh, yl = _split2(y)
    return _mm_shared(xs, (_head_blocks(yh, same_head), _head_blocks(yl, same_head)), dims, passes)


def _unit_lower_inverse(a, same_head, passes):
    c = a.shape[-2]
    ri = lax.broadcasted_iota(jnp.int32, a.shape[-2:], 0)
    ci = lax.broadcasted_iota(jnp.int32, a.shape[-2:], 1) % c
    same16 = (ri // 16) == (ci // 16)
    same32 = (ri // 32) == (ci // 32)
    eye = (ri == ci).astype(F32)
    mm = lambda us, v: _mm_heads(us, v, same_head, passes, BNN)
    a16 = jnp.where(same16, a, 0.0)
    t = eye + a16
    (p,) = mm([a16], a16)
    for _ in range(2):
        tp, p2 = mm([t, p], p)
        t = t + tp
        p = p2
    t = t + mm([t], p)[0]
    o16 = jnp.where(same32 & jnp.logical_not(same16), a, 0.0)
    t = t + mm([t], mm([o16], t)[0])[0]
    o32 = jnp.where(same32, 0.0, a)
    return t + mm([t], mm([o32], t)[0])[0]


def _rwkv_kernel(z_ref, mu_ref, pv_ref, wl_ref, y_ref, prev_ref, st_ref, *, nb, nc, passes):
    c = RWKV_CHUNK
    kd = RWKV_DIM
    g = nb * nc
    rows = nc * c
    p_score, p_inv, p_apply, p_state = passes

    @pl.when(pl.program_id(0) == 0)
    def _():
        prev_ref[...] = jnp.zeros_like(prev_ref)
        st_ref[...] = jnp.zeros_like(st_ref)

    mu = mu_ref[...]
    row0 = lax.broadcasted_iota(jnp.int32, (rows, RWKV_IN), 0) == 0
    zs = []
    for b in range(nb):
        zb = z_ref[b]
        prev = jnp.where(row0, prev_ref[b], pltpu.roll(zb, 1, 0))
        prev_ref[b] = zb[rows - 1:rows, :]
        zs.append(zb + (prev - zb) * mu)
    zf = jnp.stack(zs, axis=0).reshape(g * c, RWKV_IN)

    pv = pv_ref[...]
    w0, a0, k_k, k_a, r_k, ln_w, ln_b = (pv[i:i + 1] for i in range(7))
    r = zf[:, 0:kd]
    k = zf[:, kd:2 * kd]
    v = zf[:, 2 * kd:3 * kd]
    lo = zf[:, 3 * kd:]
    lane = lax.broadcasted_iota(jnp.int32, (1, kd), 1)
    lora_in = jnp.where(lane < 64, jnp.tanh(lo), jnp.where(lane < 128, lo, _sigmoid(lo)))
    lin = _dg(lora_in.astype(BF16), wl_ref[...], NN)
    lw = -jnp.exp(-_softplus(-(w0 + lin[:, 0:kd])) - 0.5)
    a = _sigmoid(a0 + lin[:, kd:2 * kd])
    gate = lin[:, 2 * kd:]

    hrow = lax.broadcasted_iota(jnp.int32, (kd, kd), 0) // HEAD_DIM
    hcol = lax.broadcasted_iota(jnp.int32, (kd, kd), 1) // HEAD_DIM
    same_head = hrow == hcol
    ones_bd = same_head.astype(BF16)
    head_sum = lambda t: _mm_exact_rhs(t, ones_bd, NN)

    kkr = k * k_k
    kk = kkr / jnp.maximum(jnp.sqrt(head_sum(kkr * kkr)), 1e-12)
    kp = k * (1.0 + (a - 1.0) * k_a)
    bonus = head_sum(r * kp * r_k) * v

    to3 = lambda t: t.reshape(g, c, kd)
    r3, kp3, v3, kk3, a3, lw3 = (to3(t) for t in (r, kp, v, kk, a, lw))

    lower = (lax.broadcasted_iota(jnp.int32, (c, c), 0) >= lax.broadcasted_iota(jnp.int32, (c, c), 1))
    tri = jnp.broadcast_to(lower.astype(BF16)[None], (g, c, c))
    cum = _mm_exact_lhs(tri, lw3, BNN)
    cum_end = cum[:, c - 1:c, :]
    e_in = jnp.exp(cum)
    e_out = jnp.exp(-cum)
    e_tail = jnp.exp(cum_end - cum)
    p_end = jnp.exp(cum_end)
    beta = kk3 * a3
    ab = -kk3 * jnp.exp(cum - lw3)
    rb = r3 * e_in
    bb = beta * e_out
    kb = kp3 * e_out
    bt = beta * e_tail
    kt = kp3 * e_tail

    ri = lax.broadcasted_iota(jnp.int32, (c, kd), 0)
    cj = lax.broadcasted_iota(jnp.int32, (c, kd), 1) % c
    strict = ri > cj
    incl = ri >= cj
    sab, srb = _mm_heads([ab, rb], bb, ones_bd, p_score, BNT)
    sak, srk = _mm_heads([ab, rb], kb, ones_bd, p_score, BNT)
    a_ab = jnp.where(strict, sab, 0.0)
    a_rb = jnp.where(incl, srb, 0.0)
    a_ak = jnp.where(strict, sak, 0.0)
    a_rk = jnp.where(incl, srk, 0.0)
    t_inv = _unit_lower_inverse(a_ab, ones_bd, p_inv)

    app = lambda us, w: _mm_heads(us, w, ones_bd, p_apply, BNN)
    akv, arkv = app([a_ak, a_rk], v3)
    (w_all,) = app([t_inv], ab)
    (u0_all,) = app([t_inv], akv)
    rq = rb + app([a_rb], w_all)[0]
    yc = app([a_rb], u0_all)[0] + arkv

    bt_t = jnp.swapaxes(bt, 1, 2)
    kt_t = jnp.swapaxes(kt, 1, 2)
    eye = (lax.broadcasted_iota(jnp.int32, (kd, kd), 0)
           == lax.broadcasted_iota(jnp.int32, (kd, kd), 1))
    full = lambda u, w: _mm(u, w, BNN, p_apply)
    trans = jnp.where(same_head, full(bt_t, w_all), 0.0) + jnp.where(eye, p_end, 0.0)
    inject = jnp.where(same_head, full(bt_t, u0_all) + full(kt_t, v3), 0.0)

    per_chunk = lambda t, j: t.reshape((nb, nc) + t.shape[1:])[:, j]
    st = st_ref[...]
    ys = []
    for j in range(nc):
        y_st, st_new = _mm_shared([per_chunk(rq, j), per_chunk(trans, j)], st, BNN, p_state)
        ys.append(y_st + per_chunk(yc, j))
        st = st_new + per_chunk(inject, j)
    st_ref[...] = st
    y = jnp.stack(ys, axis=1)

    y2 = y.reshape(g * c, kd)
    mean = head_sum(y2) * (1.0 / HEAD_DIM)
    dv = y2 - mean
    var = head_sum(dv * dv) * (1.0 / HEAD_DIM)
    out = (dv * lax.rsqrt(var + RWKV_GN_EPS) * ln_w + ln_b + bonus) * gate
    y_ref[...] = out.reshape(nb, rows, kd).astype(y_ref.dtype)


def rwkv_mixer(z3, mu, pvec, w_lora, *, passes=(3, 3, 3, 3), chunks_per_step=4):
    b, s, _ = z3.shape
    c = RWKV_CHUNK * chunks_per_step
    return pl.pallas_call(
        functools.partial(_rwkv_kernel, nb=b, nc=chunks_per_step, passes=passes),
        grid=(s // c,),
        in_specs=[
            pl.BlockSpec((b, c, RWKV_IN), lambda t: (0, t, 0)),
            pl.BlockSpec((1, RWKV_IN), lambda t: (0, 0)),
            pl.BlockSpec((8, RWKV_DIM), lambda t: (0, 0)),
            pl.BlockSpec((RWKV_LORA_IN, 3 * RWKV_DIM), lambda t: (0, 0)),
        ],
        out_specs=pl.BlockSpec((b, c, RWKV_DIM), lambda t: (0, t, 0)),
        out_shape=jax.ShapeDtypeStruct((b, s, RWKV_DIM), BF16),
        scratch_shapes=[
            pltpu.VMEM((b, 1, RWKV_IN), F32),
            pltpu.VMEM((b, RWKV_DIM, RWKV_DIM), F32),
        ],
        compiler_params=_params("arbitrary"),
        name="rwkv_mixer",
    )(z3, mu, pvec, w_lora)


def _pool_kernel(z_ref, w_ref, sc_ref, y_ref, ext_ref, *, tp):
    t = pl.program_id(1)
    hl = POOL_HALO
    levels = len(POOL_WINDOWS)

    @pl.when(t == 0)
    def _():
        ext_ref[:, 0:hl, :] = jnp.zeros((levels, hl, POOL_DIM), F32)

    z = z_ref[0]
    sums = []
    cur = z
    for lvl, w in enumerate(POOL_WINDOWS):
        ext_ref[lvl, hl:hl + tp, :] = cur
        back = w // 2
        cur = cur + ext_ref[lvl, hl - back:hl - back + tp, :]
        sums.append(cur)
    for lvl in range(levels):
        ext_ref[lvl, 0:hl, :] = ext_ref[lvl, tp:tp + hl, :]

    lane = lax.broadcasted_iota(jnp.int32, (tp, POOL_DIM), 1)
    pos = (t * tp + lax.broadcasted_iota(jnp.int32, (tp, POOL_DIM), 0) + 1).astype(F32)
    grp = POOL_DIM // levels
    win_sum = sums[-1]
    count = jnp.minimum(pos, float(POOL_WINDOWS[-1]))
    for lvl in range(levels - 2, -1, -1):
        in_grp = lane < (lvl + 1) * grp
        win_sum = jnp.where(in_grp, sums[lvl], win_sum)
        count = jnp.where(in_grp, jnp.minimum(pos, float(POOL_WINDOWS[lvl])), count)
    pooled = win_sum / count - z
    y = _dg(pooled.astype(BF16), w_ref[...], NN) * sc_ref[...]
    y_ref[0] = y.astype(y_ref.dtype)


def pool_mixer(zp3, w_bd, scale, *, tp):
    b, s, _ = zp3.shape
    return pl.pallas_call(
        functools.partial(_pool_kernel, tp=tp),
        grid=(b, s // tp),
        in_specs=[
            pl.BlockSpec((1, tp, POOL_DIM), lambda i, t: (i, t, 0)),
            pl.BlockSpec((POOL_DIM, POOL_DIM), lambda i, t: (0, 0)),
            pl.BlockSpec((1, POOL_DIM), lambda i, t: (0, 0)),
        ],
        out_specs=pl.BlockSpec((1, tp, POOL_DIM), lambda i, t: (i, t, 0)),
        out_shape=jax.ShapeDtypeStruct((b, s, POOL_DIM), BF16),
        scratch_shapes=[pltpu.VMEM((len(POOL_WINDOWS), POOL_HALO + tp, POOL_DIM), F32)],
        compiler_params=_params("arbitrary", "arbitrary"),
        name="pool_mixer",
    )(zp3, w_bd, scale)


def _fox_kernel(q_ref, k_ref, vt_ref, cc_ref, o_ref, ka_ref, m_ref, l_ref, acc_ref, sa_ref, sb_ref,
                *, tq, nblk):
    hp = pl.program_id(1)
    i = pl.program_id(2)
    lane = lax.broadcasted_iota(jnp.int32, (1, LANES), 1)
    spare = [HEAD_DIM * (1 - hh) for hh in range(2)]

    @pl.when(i == 0)
    def _():
        def fill(jb, carry):
            start = pl.multiple_of(jb * tq, tq)
            blk = cc_ref[0, pl.ds(start, tq), :]
            k2 = k_ref[0, pl.ds(start, tq), :]
            for hh in range(2):
                col = jnp.sum(jnp.where(lane == 2 * hp + hh, blk, 0.0), axis=-1, keepdims=True)
                hi, mid, lo = _split3(col * (-LOG2_E))
                ka = k2
                for off, part in enumerate((hi, mid, lo)):
                    ka = jnp.where(lane == spare[hh] + off, part, ka)
                ka_ref[hh, pl.ds(start, tq), :] = ka
            return carry

        lax.fori_loop(0, nblk, fill, 0)

    q2 = q_ref[0]
    m_ref[...] = jnp.full(m_ref.shape, NEG_BIG, F32)
    l_ref[...] = jnp.zeros(l_ref.shape, F32)
    acc_ref[...] = jnp.zeros(acc_ref.shape, F32)
    units = [(lane >= spare[hh]) & (lane < spare[hh] + 3) for hh in range(2)]
    qh = [jnp.where(lane // HEAD_DIM == hh, q2, units[hh].astype(q2.dtype)) for hh in range(2)]

    def scores_into(j, dst):
        start = pl.multiple_of(j * tq, tq)
        for hh in range(2):
            dst[hh] = _dg(ka_ref[hh, pl.ds(start, tq), :], qh[hh], NT)

    def consume(j, src, masked):
        for hh in range(2):
            s = src[hh]
            if masked:
                kpos = lax.broadcasted_iota(jnp.int32, (tq, tq), 0)
                qpos = lax.broadcasted_iota(jnp.int32, (tq, tq), 1)
                s = jnp.where(kpos <= qpos, s, NEG_BIG)
            m_old = m_ref[hh][0:1, :]
            m_new = jnp.maximum(m_old, jnp.max(s, axis=0, keepdims=True))
            alpha = jnp.exp2(m_old - m_new)
            p = jnp.exp2(s - m_new)
            l_new = alpha * l_ref[hh][0:1, :] + jnp.sum(p, axis=0, keepdims=True)
            vth = vt_ref[0, j, hh * HEAD_DIM:(hh + 1) * HEAD_DIM, :]
            acc_ref[hh] = alpha * acc_ref[hh] + _dg(vth, p.astype(BF16), NN)
            l_ref[hh] = jnp.broadcast_to(l_new, (8, tq))
            m_ref[hh] = jnp.broadcast_to(m_new, (8, tq))

    scores_into(0, sa_ref)

    def pair(jj, carry):
        j = 2 * jj
        scores_into(j + 1, sb_ref)
        consume(j, sa_ref, False)
        scores_into(j + 2, sa_ref)
        consume(j + 1, sb_ref, False)
        return carry

    lax.fori_loop(0, i // 2, pair, 0)

    @pl.when(i % 2 == 0)
    def _():
        consume(i, sa_ref, True)

    @pl.when(i % 2 == 1)
    def _():
        scores_into(i, sb_ref)
        consume(i - 1, sa_ref, False)
        consume(i, sb_ref, True)

    ot = jnp.concatenate([acc_ref[hh] / l_ref[hh][0:1, :] for hh in range(2)], axis=0)
    o_ref[0] = ot.T.astype(o_ref.dtype)


def fox_attention(q3, k3, vt4, cumc3, *, tq):
    b, s, _ = q3.shape
    nq = s // tq
    pairs = FOX_HEADS // 2
    return pl.pallas_call(
        functools.partial(_fox_kernel, tq=tq, nblk=nq),
        grid=(b, pairs, nq),
        in_specs=[
            pl.BlockSpec((1, tq, LANES), lambda bi, hp, i: (bi, i, hp)),
            pl.BlockSpec((1, s, LANES), lambda bi, hp, i: (bi, 0, hp)),
            pl.BlockSpec((1, nq, LANES, tq), lambda bi, hp, i: (bi, 0, hp, 0)),
            pl.BlockSpec((1, s, LANES), lambda bi, hp, i: (bi, 0, 0)),
        ],
        out_specs=pl.BlockSpec((1, tq, LANES), lambda bi, hp, i: (bi, i, hp)),
        out_shape=jax.ShapeDtypeStruct((b, s, FOX_DIM), BF16),
        scratch_shapes=[
            pltpu.VMEM((2, s, LANES), BF16),
            pltpu.VMEM((2, 8, tq), F32),
            pltpu.VMEM((2, 8, tq), F32),
            pltpu.VMEM((2, HEAD_DIM, tq), F32),
            pltpu.VMEM((2, tq, tq), F32),
            pltpu.VMEM((2, tq, tq), F32),
        ],
        compiler_params=_params("arbitrary", "arbitrary", "arbitrary"),
        name="fox_attention",
    )(q3, k3, vt4, cumc3)


def _outproj_kernel(yr_ref, yp_ref, yf_ref, x_ref, mod_ref, wo_ref, n2_ref, rw_ref, rb_ref,
                    x1_ref, h2_ref, ti_ref, tg_ref):
    mod = mod_ref[0]
    mixed = (_dg(yr_ref[...], wo_ref[0:RWKV_DIM, :], NN)
             + _dg(yp_ref[...], wo_ref[RWKV_DIM:RWKV_DIM + POOL_DIM, :], NN)
             + _dg(yf_ref[...], wo_ref[RWKV_DIM + POOL_DIM:, :], NN))
    x1 = x_ref[...] + mod[2:3] * mixed
    x1_ref[...] = x1
    h2 = _rms_modulate(x1, n2_ref[...], mod[4:5], mod[3:4])
    h2_ref[...] = h2

    tm = x1.shape[0]
    lane = lax.broadcasted_iota(jnp.int32, (tm, LANES), 1)
    lane_f = lane.astype(F32)
    logits = _mm(h2, rw_ref[...], NN, passes=3) + rb_ref[...]
    work = jnp.where(lane < N_EXPERTS, logits, NEG_BIG)
    top_i = jnp.zeros((tm, LANES), F32)
    top_e = jnp.zeros((tm, LANES), F32)
    v0 = None
    for kk in range(TOP_K):
        vmax = jnp.max(work, axis=-1, keepdims=True)
        idx = jnp.min(jnp.where(work == vmax, lane_f, float(LANES)), axis=-1, keepdims=True)
        if kk == 0:
            v0 = vmax
        top_i = jnp.where(lane == kk, idx, top_i)
        top_e = jnp.where(lane == kk, jnp.exp(vmax - v0), top_e)
        work = jnp.where(lane_f == idx, NEG_BIG, work)
    ti_ref[...] = top_i.astype(jnp.int32)
    tg_ref[...] = top_e / jnp.sum(top_e, axis=-1, keepdims=True)


def out_projection(yr, yp, yf, x2, mod, w_out, norm2, rw, rb, *, seq, tm):
    n, d = x2.shape
    tpb = seq // tm
    row = lambda i: (i, 0)
    const = lambda i: (0, 0)
    return pl.pallas_call(
        _outproj_kernel,
        grid=(n // tm,),
        in_specs=[
            pl.BlockSpec((tm, RWKV_DIM), row),
            pl.BlockSpec((tm, POOL_DIM), row),
            pl.BlockSpec((tm, FOX_DIM), row),
            pl.BlockSpec((tm, d), row),
            pl.BlockSpec((1, 6, d), lambda i: (i // tpb, 0, 0)),
            pl.BlockSpec((d, d), const),
            pl.BlockSpec((1, d), const),
            pl.BlockSpec((d, LANES), const),
            pl.BlockSpec((1, LANES), const),
        ],
        out_specs=(
            pl.BlockSpec((tm, d), row),
            pl.BlockSpec((tm, d), row),
            pl.BlockSpec((tm, LANES), row),
            pl.BlockSpec((tm, LANES), row),
        ),
        out_shape=(
            jax.ShapeDtypeStruct((n, d), F32),
            jax.ShapeDtypeStruct((n, d), F32),
            jax.ShapeDtypeStruct((n, LANES), jnp.int32),
            jax.ShapeDtypeStruct((n, LANES), F32),
        ),
        compiler_params=_params("arbitrary"),
        name="out_projection",
    )(yr, yp, yf, x2, mod, w_out, norm2, rw, rb)


def _rank_kernel(ti_ref, rank_ref, cnt_ref, carry_ref, *, tr):
    @pl.when(pl.program_id(0) == 0)
    def _():
        carry_ref[...] = jnp.zeros_like(carry_ref)

    ti = ti_ref[...]
    lane = lax.broadcasted_iota(jnp.int32, (tr, LANES), 1)
    hits = [ti[:, kk:kk + 1] == lane for kk in range(TOP_K)]
    cnt = hits[0].astype(F32)
    for kk in range(1, TOP_K):
        cnt = cnt + hits[kk].astype(F32)
    before = (lax.broadcasted_iota(jnp.int32, (tr, tr), 0)
              > lax.broadcasted_iota(jnp.int32, (tr, tr), 1)).astype(BF16)
    pre = _dg(before, cnt.astype(BF16), NN) + carry_ref[...]
    rank = jnp.zeros((tr, LANES), F32)
    for kk in range(TOP_K):
        rk = jnp.sum(jnp.where(hits[kk], pre, 0.0), axis=-1, keepdims=True)
        rank = jnp.where(lane == kk, rk, rank)
    rank_ref[...] = rank.astype(jnp.int32)
    total = carry_ref[...] + jnp.sum(cnt, axis=0, keepdims=True)
    carry_ref[...] = total
    cnt_ref[...] = jnp.broadcast_to(total, cnt_ref.shape)


def moe_rank(top_i, *, tr):
    n = top_i.shape[0]
    return pl.pallas_call(
        functools.partial(_rank_kernel, tr=tr),
        grid=(n // tr,),
        in_specs=[pl.BlockSpec((tr, LANES), lambda i: (i, 0))],
        out_specs=(pl.BlockSpec((tr, LANES), lambda i: (i, 0)),
                   pl.BlockSpec((8, LANES), lambda i: (0, 0))),
        out_shape=(jax.ShapeDtypeStruct((n, LANES), jnp.int32),
                   jax.ShapeDtypeStruct((8, LANES), F32)),
        scratch_shapes=[pltpu.VMEM((1, LANES), F32)],
        compiler_params=_params("arbitrary"),
        name="moe_rank",
    )(top_i)


def _routing_tables(top_i, *, rows_per_block, tr):
    n = top_i.shape[0]
    r = rows_per_block
    rank, cnt = moe_rank(top_i, tr=tr)
    counts = cnt[0, :N_EXPERTS].astype(jnp.int32)
    padded = (counts + r - 1) // r * r
    pend = jnp.cumsum(padded)
    pstart = pend - padded
    e_sel = top_i[:, :TOP_K]
    onehot = e_sel[:, :, None] == jnp.arange(N_EXPERTS, dtype=jnp.int32)[None, None, :]
    dest = (jnp.sum(jnp.where(onehot, pstart[None, None, :], 0), axis=-1) + rank[:, :TOP_K]).astype(jnp.int32)
    nb = -(-(n * TOP_K + N_EXPERTS * (r - 1)) // r)
    bstart = jnp.arange(nb, dtype=jnp.int32) * r
    block_e = jnp.minimum(jnp.sum(bstart[:, None] >= pend[None, :], axis=1), N_EXPERTS - 1).astype(jnp.int32)
    in_e = block_e[:, None] == jnp.arange(N_EXPERTS, dtype=jnp.int32)[None, :]
    seg_end = jnp.sum(jnp.where(in_e, (pstart + counts)[None, :], 0), axis=1)
    nvalid = jnp.clip(seg_end - bstart, 0, r).astype(jnp.int32)
    nused = (pend[-1] // r).astype(jnp.int32).reshape(1)
    pad = jnp.stack([pstart + counts, padded - counts]).astype(jnp.int32)
    return dest, block_e, nvalid, nused, pad, nb


def _zero_fill(pad_ref, nu_ref, xs_ref, zero_ref, zsem, *, r, nb):
    row_sem, block_sem = zsem.at[0], zsem.at[1]
    zero_ref[...] = jnp.zeros_like(zero_ref)

    def each_copy(fn):
        def per_expert(e, carry):
            start = pad_ref[0, e]

            def per_row(i, c2):
                fn(pltpu.make_async_copy(zero_ref.at[pl.ds(0, 1)], xs_ref.at[pl.ds(start + i, 1)], row_sem))
                return c2

            lax.fori_loop(0, pad_ref[1, e], per_row, 0)
            return carry

        lax.fori_loop(0, N_EXPERTS, per_expert, 0)

        def per_block(b, carry):
            fn(pltpu.make_async_copy(zero_ref, xs_ref.at[pl.ds(pl.multiple_of(b * r, r), r)], block_sem))
            return carry

        lax.fori_loop(nu_ref[0], nb, per_block, 0)

    each_copy(lambda cp: cp.start())
    each_copy(lambda cp: cp.wait())


def _dispatch_kernel(dest_ref, pad_ref, nu_ref, h_ref, xs_ref, zero_ref, hb0, hb1, sem, zsem, *, td, r, nb, nt):
    i = pl.program_id(0)
    hbufs = (hb0, hb1)

    @pl.when(i == 0)
    def _():
        _zero_fill(pad_ref, nu_ref, xs_ref, zero_ref, zsem, r=r, nb=nb)

    def drain(half):
        for kk in range(TOP_K):
            pltpu.make_async_copy(hbufs[half], xs_ref.at[pl.ds(0, td)], sem.at[half]).wait()

    def step(cur):
        hbufs[cur][...] = h_ref[...]
        for t in range(td):
            for kk in range(TOP_K):
                pltpu.make_async_copy(hbufs[cur].at[pl.ds(t, 1)],
                                      xs_ref.at[pl.ds(dest_ref[0, 0, TOP_K * t + kk], 1)], sem.at[cur]
                                      ).start(priority=kk % 2)

        @pl.when(i >= 1)
        def _():
            drain(1 - cur)

        @pl.when(i == nt - 1)
        def _():
            drain(cur)

    for parity in range(2):
        pl.when(i % 2 == parity)(functools.partial(step, parity))


def moe_dispatch(h2, dest2, pad, nused, *, nb, r, td):
    n, d = h2.shape
    return pl.pallas_call(
        functools.partial(_dispatch_kernel, td=td, r=r, nb=nb, nt=n // td),
        grid=(n // td,),
        in_specs=[
            pl.BlockSpec((1, 1, TOP_K * td), lambda i: (i, 0, 0), memory_space=pltpu.SMEM),
            pl.BlockSpec(memory_space=pltpu.SMEM),
            pl.BlockSpec(memory_space=pltpu.SMEM),
            pl.BlockSpec((td, d), lambda i: (i, 0)),
        ],
        out_specs=pl.BlockSpec(memory_space=pl.ANY),
        out_shape=jax.ShapeDtypeStruct((nb * r, d), F32),
        scratch_shapes=[pltpu.VMEM((r, d), F32), pltpu.VMEM((td, d), F32), pltpu.VMEM((td, d), F32),
                        pltpu.SemaphoreType.DMA((2,)), pltpu.SemaphoreType.DMA((2,))],
        compiler_params=_params("arbitrary"),
        name="moe_dispatch",
    )(dest2, pad, nused, h2)


def _expert_kernel(be_ref, nv_ref, nu_ref, x_ref, wgu_ref, bgu_ref, wd_ref, bd_ref, y_ref,
                   wgu_bf, wd_bf, cur_ref):
    b = pl.program_id(0)
    e = be_ref[b]
    nv = nv_ref[b]

    @pl.when(b == 0)
    def _():
        cur_ref[0] = -1

    @pl.when((nv > 0) & (cur_ref[0] != e))
    def _():
        cur_ref[0] = e
        step = 128

        def cast(i, carry):
            s = pl.multiple_of(i * step, step)
            wgu_bf[pl.ds(s, step), :] = wgu_ref[0, 0, pl.ds(s, step), :].astype(BF16)
            wd_bf[pl.ds(s, step), :] = wd_ref[0, 0, pl.ds(s, step), :].astype(BF16)
            return carry

        lax.fori_loop(0, D_MODEL // step, cast, 0)

    @pl.when(nv == 0)
    def _():
        y_ref[...] = jnp.zeros_like(y_ref)

    @pl.when(nv > 0)
    def _():
        x = x_ref[...].astype(BF16)
        gu = _dg(x, wgu_bf[...], NN) + bgu_ref[0, 0]
        gate = jnp.minimum(gu[:, :D_FF], SWIGLU_LIMIT)
        up = jnp.clip(gu[:, D_FF:], -SWIGLU_LIMIT, SWIGLU_LIMIT)
        act = (up + 1.0) * (gate * _sigmoid(SWIGLU_ALPHA * gate))
        y_ref[...] = _dg(act.astype(BF16), wd_bf[...], NN) + bd_ref[0, 0]


def moe_experts(xs, block_e, nvalid, nused, w_gu, b_gu, w_down, b_down, *, layer, r):
    rows, d = xs.shape
    nb = rows // r
    blk = lambda b, be, nv, nu: (jnp.minimum(b, nu[0] - 1), 0)
    wsel = lambda b, be, nv, nu: (layer, be[b], 0, 0)
    grid_spec = pltpu.PrefetchScalarGridSpec(
        num_scalar_prefetch=3,
        grid=(nb,),
        in_specs=[
            pl.BlockSpec((r, d), blk),
            pl.BlockSpec((1, 1, d, 2 * D_FF), wsel),
            pl.BlockSpec((1, 1, 1, 2 * D_FF), wsel),
            pl.BlockSpec((1, 1, D_FF, d), wsel),
            pl.BlockSpec((1, 1, 1, d), wsel),
        ],
        out_specs=pl.BlockSpec((r, d), lambda b, be, nv, nu: (b, 0)),
        scratch_shapes=[
            pltpu.VMEM((d, 2 * D_FF), BF16),
            pltpu.VMEM((D_FF, d), BF16),
            pltpu.SMEM((1,), jnp.int32),
        ],
    )
    return pl.pallas_call(
        _expert_kernel,
        grid_spec=grid_spec,
        out_shape=jax.ShapeDtypeStruct((rows, d), F32),
        compiler_params=_params("arbitrary"),
        name="moe_experts",
    )(block_e, nvalid, nused, xs, w_gu, b_gu, w_down, b_down)


def _combine_kernel(dest0_ref, destn_ref, yb_ref, tg_ref, x_ref, mod_ref, nf_ref, o_ref, buf0, buf1, sem,
                    *, tc, nt, final_norm):
    i = pl.program_id(0)
    bufs = (buf0, buf1)

    def gather(tab_ref, half):
        for t in range(tc):
            for kk in range(TOP_K):
                pltpu.make_async_copy(yb_ref.at[pl.ds(tab_ref[0, 0, TOP_K * t + kk], 1)],
                                      bufs[half].at[kk, pl.ds(t, 1)], sem.at[half]).start(priority=kk % 2)

    def gather_wait(half):
        for kk in range(TOP_K):
            pltpu.make_async_copy(yb_ref.at[pl.ds(0, tc)], bufs[half].at[kk], sem.at[half]).wait()

    @pl.when(i == 0)
    def _():
        gather(dest0_ref, 0)

    def step(cur):
        @pl.when(i + 1 < nt)
        def _():
            gather(destn_ref, 1 - cur)

        gather_wait(cur)
        _combine_tile(bufs[cur], tg_ref, x_ref, mod_ref, nf_ref, o_ref, final_norm=final_norm)

    for parity in range(2):
        pl.when(i % 2 == parity)(functools.partial(step, parity))


def _combine_tile(buf, tg_ref, x_ref, mod_ref, nf_ref, o_ref, *, final_norm):
    tg = tg_ref[...]
    y = tg[:, 0:1] * buf[0]
    for kk in range(1, TOP_K):
        y = y + tg[:, kk:kk + 1] * buf[kk]
    x = x_ref[...] + mod_ref[0][5:6] * y
    if final_norm:
        ms = jnp.mean(x * x, axis=-1, keepdims=True)
        x = x * lax.rsqrt(ms + NORM_EPS) * nf_ref[...]
    o_ref[...] = x


def moe_combine(yb, dest2, tg, x1, mod, norm_f, *, seq, tc, final_norm):
    n, d = x1.shape
    tpb = seq // tc
    nt = n // tc
    row = lambda i: (i, 0)
    return pl.pallas_call(
        functools.partial(_combine_kernel, tc=tc, nt=nt, final_norm=final_norm),
        grid=(nt,),
        in_specs=[
            pl.BlockSpec((1, 1, TOP_K * tc), lambda i: (0, 0, 0), memory_space=pltpu.SMEM),
            pl.BlockSpec((1, 1, TOP_K * tc), lambda i: (jnp.minimum(i + 1, nt - 1), 0, 0),
                         memory_space=pltpu.SMEM),
            pl.BlockSpec(memory_space=pl.ANY),
            pl.BlockSpec((tc, LANES), row),
            pl.BlockSpec((tc, d), row),
            pl.BlockSpec((1, 6, d), lambda i: (i // tpb, 0, 0)),
            pl.BlockSpec((1, d), lambda i: (0, 0)),
        ],
        out_specs=pl.BlockSpec((tc, d), row),
        out_shape=jax.ShapeDtypeStruct((n, d), F32),
        scratch_shapes=[pltpu.VMEM((TOP_K, tc, d), F32), pltpu.VMEM((TOP_K, tc, d), F32),
                        pltpu.SemaphoreType.DMA((2,))],
        compiler_params=_params("arbitrary"),
        name="moe_combine",
    )(dest2, dest2, yb, tg, x1, mod, norm_f)


def _block_diag(blocks):
    g, m, _ = blocks.shape
    out = jnp.zeros((g * m, g * m), blocks.dtype)
    for i in range(g):
        out = out.at[i * m:(i + 1) * m, i * m:(i + 1) * m].set(blocks[i])
    return out


def _pad_cols(w, cols):
    return jnp.zeros(w.shape[:-1] + (cols,), w.dtype).at[..., :w.shape[-1]].set(w)


def trunk(x, c, ada_w, ada_b, norm1, w_in, w_out, rwkv_mu, rwkv_w0, rwkv_w_up, rwkv_a0, rwkv_a_up,
          rwkv_g_up, rwkv_k_k, rwkv_k_a, rwkv_r_k, rwkv_ln_w, rwkv_ln_b, pool_w, pool_scale, fox_b_f,
          norm2, router_w, router_b, moe_w_gu, moe_b_gu, moe_w_down, moe_b_down, norm_f,
          *, tm=512, tp=512, tq=512, expert_rows=256, td=256, tc=256, rwkv_passes=(1, 3, 1, 3)):
    batch, seq, d = x.shape
    depth = ada_w.shape[0]
    n = batch * seq
    mod_all = ada_modulation(c, ada_w, ada_b)
    x2 = x.reshape(n, d)
    nf = norm_f.reshape(1, d)
    b_gu4 = moe_b_gu.reshape(depth, N_EXPERTS, 1, 2 * D_FF)
    b_down4 = moe_b_down.reshape(depth, N_EXPERTS, 1, d)
    for l in range(depth):
        mod = mod_all[l]
        w_main = w_in[l][:, :MAIN_COLS].astype(BF16)
        w_f = _pad_cols(w_in[l][:, MAIN_COLS:], LANES)
        b_f = _pad_cols(fox_b_f[l].reshape(1, FOX_HEADS), LANES)
        w_lora = jnp.zeros((RWKV_LORA_IN, 3 * RWKV_DIM), F32)
        w_lora = w_lora.at[0:64, 0:RWKV_DIM].set(rwkv_w_up[l])
        w_lora = w_lora.at[64:128, RWKV_DIM:2 * RWKV_DIM].set(rwkv_a_up[l])
        w_lora = w_lora.at[128:256, 2 * RWKV_DIM:].set(rwkv_g_up[l]).astype(BF16)
        pvec = jnp.stack([rwkv_w0[l], rwkv_a0[l], rwkv_k_k[l], rwkv_k_a[l], rwkv_r_k[l].reshape(-1),
                          rwkv_ln_w[l], rwkv_ln_b[l], jnp.zeros((RWKV_DIM,), F32)], axis=0)
        w_pool = _block_diag(pool_w[l]).astype(BF16)
        rw = _pad_cols(router_w[l], LANES)
        rb = _pad_cols(router_b[l].reshape(1, N_EXPERTS), LANES)

        zr, zp, q, k, vt, cumc = in_projection(
            x2, mod, norm1[l].reshape(1, d), w_main, w_f, b_f, batch=batch, seq=seq, tm=tm, tv=tq)
        yr = rwkv_mixer(zr.reshape(batch, seq, RWKV_IN), rwkv_mu[l].reshape(1, RWKV_IN), pvec, w_lora,
                        passes=rwkv_passes)
        yp = pool_mixer(zp.reshape(batch, seq, POOL_DIM), w_pool, pool_scale[l].reshape(1, POOL_DIM), tp=tp)
        to3 = lambda t: t.reshape(batch, seq, t.shape[-1])
        yf = fox_attention(to3(q), to3(k), vt, to3(cumc), tq=tq)
        x1, h2, top_i, top_g = out_projection(
            yr.reshape(n, RWKV_DIM), yp.reshape(n, POOL_DIM), yf.reshape(n, FOX_DIM), x2, mod,
            w_out[l].astype(BF16), norm2[l].reshape(1, d), rw, rb, seq=seq, tm=tm)

        dest, block_e, nvalid, nused, pad, nb = _routing_tables(top_i, rows_per_block=expert_rows, tr=tm)
        xs = moe_dispatch(h2, dest.reshape(n // td, 1, TOP_K * td), pad, nused, nb=nb, r=expert_rows, td=td)
        yb = moe_experts(xs, block_e, nvalid, nused, moe_w_gu, b_gu4, moe_w_down, b_down4,
                         layer=l, r=expert_rows)
        x2 = moe_combine(yb, dest.reshape(n // tc, 1, TOP_K * tc), top_g, x1, mod, nf, seq=seq, tc=tc,
                         final_norm=(l == depth - 1))
    return x2.reshape(batch, seq, d)


def kernel(x, c, ada_w, ada_b, norm1, w_in, w_out, rwkv_mu, rwkv_w0, rwkv_w_up, rwkv_a0, rwkv_a_up, rwkv_g_up, rwkv_k_k, rwkv_k_a, rwkv_r_k, rwkv_ln_w, rwkv_ln_b, pool_w, pool_scale, fox_b_f, norm2, router_w, router_b, moe_w_gu, moe_b_gu, moe_w_down, moe_b_down, norm_f):
    return trunk(x, c, ada_w, ada_b, norm1, w_in, w_out, rwkv_mu, rwkv_w0, rwkv_w_up, rwkv_a0, rwkv_a_up,
                 rwkv_g_up, rwkv_k_k, rwkv_k_a, rwkv_r_k, rwkv_ln_w, rwkv_ln_b, pool_w, pool_scale, fox_b_f,
                 norm2, router_w, router_b, moe_w_gu, moe_b_gu, moe_w_down, moe_b_down, norm_f)
```
